```python
import jax, jax.numpy as jnp
from jax import lax
import numpy as np

D_MODEL = 1024
BATCH = 2
SEQ = 16384
DEPTH = 4

GRID_W = 64
CTX_LEN = 256
HEAD_DIM = 64
RET_HEADS = 4
RET_W = RET_HEADS * HEAD_DIM
ATT_Q_HEADS = 8
ATT_KV_HEADS = 2
ATT_GROUP = ATT_Q_HEADS // ATT_KV_HEADS
ATT_W = ATT_Q_HEADS * HEAD_DIM
KV_W = ATT_KV_HEADS * HEAD_DIM
CM_GROUPS = 4
CM_CH = 64
CM_W = CM_GROUPS * CM_CH
MIX_W = RET_W + ATT_W + CM_W
IN_SPLITS = (RET_W, RET_W, RET_W, RET_W, ATT_W, KV_W, KV_W, CM_W, CM_W)
IN_W = 4 * RET_W + ATT_W + 2 * KV_W + 2 * CM_W
CHUNK = 128
WINDOW = 128
BLOCK = 128
FFN_HID = -(-8 * D_MODEL // (3 * 256)) * 256
ROPE_BASE = 10000.0
AX_PAIRS = HEAD_DIM // 4
EPS = 1e-6

kernel_name = 'hymba_style_retention_swa_gmlp_dit_trunk'

F32 = jnp.float32


def rms_norm(x, g):
    xf = x.astype(F32)
    y = xf * lax.rsqrt(jnp.mean(xf * xf, axis=-1, keepdims=True) + EPS)
    return (y * g.astype(F32)).astype(x.dtype)


def modulate(z, shift, scale):
    return z * (1 + scale) + shift


def split_in(p):
    return jnp.split(p, np.cumsum(IN_SPLITS)[:-1].tolist(), axis=-1)


def axial_rope_tables(n):
    rows = n // GRID_W
    row = jnp.repeat(jnp.arange(rows, dtype=F32), GRID_W)
    col = jnp.tile(jnp.arange(GRID_W, dtype=F32), rows)
    inv = 1.0 / (ROPE_BASE ** (jnp.arange(AX_PAIRS, dtype=F32) / AX_PAIRS))
    ang = jnp.stack([row[:, None] * inv, col[:, None] * inv], axis=1)
    return jnp.cos(ang), jnp.sin(ang)


def apply_rope(x, cos, sin):
    xr = x.astype(F32).reshape(*x.shape[:-1], 2, 2, AX_PAIRS)
    x1, x2 = xr[..., 0, :], xr[..., 1, :]
    cs, sn = cos[:, None], sin[:, None]
    out = jnp.stack([x1 * cs - x2 * sn, x2 * cs + x1 * sn], axis=-2)
    return out.reshape(x.shape).astype(x.dtype)


def retention_scan(q, k, v, log_gamma, s0):
    b, h, n, dk = q.shape
    dv = v.shape[-1]
    nc = n // CHUNK
    to_chunks = lambda t: t.reshape(b, h, nc, CHUNK, t.shape[-1]).transpose(2, 0, 1, 3, 4)
    qc, kc, vc = to_chunks(q), to_chunks(k), to_chunks(v)
    idx = jnp.arange(CHUNK, dtype=F32)
    lg = log_gamma[:, None]
    diff = idx[:, None] - idx[None, :]
    intra = jnp.where(diff >= 0, jnp.exp(lg[:, :, None] * jnp.maximum(diff, 0.0)), 0.0)
    q_decay = jnp.exp(lg * (idx + 1.0))[:, :, None]
    k_decay = jnp.exp(lg * (CHUNK - 1.0 - idx))[:, :, None]
    chunk_decay = jnp.exp(lg[:, 0] * CHUNK)[:, None, None]

    def step(s, inp):
        qi, ki, vi = inp
        sc = jnp.einsum('bhqd,bhkd->bhqk', qi, ki) * intra
        o = jnp.einsum('bhqk,bhkv->bhqv', sc, vi) + jnp.einsum('bhqd,bhdv->bhqv', qi * q_decay, s)
        s = s * chunk_decay + jnp.einsum('bhkd,bhkv->bhdv', ki * k_decay, vi)
        return s, o

    s_final, o = lax.scan(step, s0, (qc, kc, vc))
    return o.transpose(1, 2, 0, 3, 4).reshape(b, h, n, dv), s_final


def retention_mixer(q, k, v, gate, decay_f, decay_b, norm_g, s0_f, s0_b, rope):
    b, n, _ = q.shape
    shp = (b, n, RET_HEADS, HEAD_DIM)
    q, k, v = q.reshape(shp), k.reshape(shp) * (HEAD_DIM ** -0.5), v.reshape(shp)
    if rope is not None:
        q, k = apply_rope(q, *rope), apply_rope(k, *rope)
    q, k, v = (t.astype(F32).transpose(0, 2, 1, 3) for t in (q, k, v))
    lg_f = jax.nn.log_sigmoid(decay_f.astype(F32))
    lg_b = jax.nn.log_sigmoid(decay_b.astype(F32))
    o_f, st_f = retention_scan(q, k, v, lg_f, s0_f)
    o_b, st_b = retention_scan(q[:, :, ::-1], k[:, :, ::-1], v[:, :, ::-1], lg_b, s0_b)
    o = (o_f + o_b[:, :, ::-1]).transpose(0, 2, 1, 3)
    mu = jnp.mean(o, axis=-1, keepdims=True)
    var = jnp.mean(jnp.square(o - mu), axis=-1, keepdims=True)
    o = ((o - mu) * lax.rsqrt(var + EPS)).reshape(b, n, RET_W) * norm_g.astype(F32)
    return (jax.nn.silu(gate.astype(F32)) * o).astype(gate.dtype), st_f, st_b


def window_attention(q, k, v, kc, vc, sink):
    b, n, _, d = q.shape
    nb = n // BLOCK
    scale = d ** -0.5
    pad = ((0, 0), (BLOCK, BLOCK), (0, 0), (0, 0))
    kp, vp = jnp.pad(k, pad), jnp.pad(v, pad)
    kpos = jnp.arange(3 * BLOCK) - BLOCK
    band = jnp.abs(kpos[None, :] - jnp.arange(BLOCK)[:, None]) <= WINDOW
    sink_l = jnp.broadcast_to(sink.astype(F32).reshape(ATT_KV_HEADS, ATT_GROUP)[None, :, :, None, None],
                              (b, ATT_KV_HEADS, ATT_GROUP, BLOCK, 1))

    def one_block(i):
        start = i * BLOCK
        qi = lax.dynamic_slice_in_dim(q, start, BLOCK, 1).reshape(b, BLOCK, ATT_KV_HEADS, ATT_GROUP, d)
        ki = lax.dynamic_slice_in_dim(kp, start, 3 * BLOCK, 1)
        vi = lax.dynamic_slice_in_dim(vp, start, 3 * BLOCK, 1)
        absk = start + kpos
        valid = band & ((absk >= 0) & (absk < n))[None, :]
        s_loc = jnp.einsum('bqhgd,bkhd->bhgqk', qi, ki).astype(F32) * scale
        s_loc = jnp.where(valid, s_loc, -jnp.inf)
        s_ctx = jnp.einsum('bqhgd,bchd->bhgqc', qi, kc).astype(F32) * scale
        p = jax.nn.softmax(jnp.concatenate([s_loc, s_ctx, sink_l], axis=-1), axis=-1).astype(v.dtype)
        return (jnp.einsum('bhgqk,bkhd->bqhgd', p[..., :3 * BLOCK], vi)
                + jnp.einsum('bhgqc,bchd->bqhgd', p[..., 3 * BLOCK:-1], vc))

    out = lax.map(one_block, jnp.arange(nb))
    return jnp.moveaxis(out, 0, 1).reshape(b, n, ATT_W)


def context_attention(q, k, v, sink):
    b, m, _, d = q.shape
    qg = q.reshape(b, m, ATT_KV_HEADS, ATT_GROUP, d)
    s = jnp.einsum('bqhgd,bkhd->bhgqk', qg, k).astype(F32) * (d ** -0.5)
    sink_l = jnp.broadcast_to(sink.astype(F32).reshape(ATT_KV_HEADS, ATT_GROUP)[None, :, :, None, None],
                              (b, ATT_KV_HEADS, ATT_GROUP, m, 1))
    p = jax.nn.softmax(jnp.concatenate([s, sink_l], axis=-1), axis=-1)[..., :-1].astype(v.dtype)
    return jnp.einsum('bhgqk,bkhd->bqhgd', p, v).reshape(b, m, ATT_W)


def chunk_gating(u, v, norm_g, w_s, b_s):
    b, n, w = u.shape
    u = jax.nn.gelu(u.astype(F32))
    vg = jax.nn.gelu(v.astype(F32)).reshape(b, n // CHUNK, CHUNK, CM_GROUPS, CM_CH)
    mu = jnp.mean(vg, axis=-1, keepdims=True)
    var = jnp.mean(jnp.square(vg - mu), axis=-1, keepdims=True)
    vn = (vg - mu) * lax.rsqrt(var + EPS) * norm_g.astype(F32).reshape(CM_GROUPS, CM_CH)
    s = jnp.einsum('gpq,bcqgd->bcpgd', w_s.astype(F32), vn) + b_s.astype(F32).T[:, :, None]
    return (u * s.reshape(b, n, w)).astype(v.dtype)


def swiglu(z, wg, wu, wd):
    return (jax.nn.silu(z @ wg) * (z @ wu)) @ wd


def setup_inputs(seed: int = 0) -> dict:
    key = jax.random.key(seed)
    ks = jax.random.split(key, 22)
    nrm = lambda k, shape, s: jax.random.normal(k, shape, F32) * s
    base_logit = jnp.log(jnp.exp2(5.0 + jnp.arange(RET_HEADS, dtype=F32)) - 1.0)
    return {
        'x': nrm(ks[0], (BATCH, SEQ, D_MODEL), 1.0),
        'c': nrm(ks[1], (BATCH, D_MODEL), 1.0),
        'ctx': nrm(ks[2], (BATCH, CTX_LEN, D_MODEL), 1.0),
        'c_ctx': nrm(ks[3], (D_MODEL,), 1.0),
        'w_mod': nrm(ks[4], (DEPTH, D_MODEL, 6 * D_MODEL), 0.5 * D_MODEL ** -0.5),
        'b_mod': nrm(ks[5], (DEPTH, 6 * D_MODEL), 0.02),
        'norm1_g': 1.0 + nrm(ks[6], (DEPTH, D_MODEL), 0.02),
        'norm2_g': 1.0 + nrm(ks[7], (DEPTH, D_MODEL), 0.02),
        'w_in': nrm(ks[8], (DEPTH, D_MODEL, IN_W), D_MODEL ** -0.5),
        'ret_decay_f': base_logit + nrm(ks[9], (DEPTH, RET_HEADS), 0.1),
        'ret_decay_b': base_logit + nrm(ks[10], (DEPTH, RET_HEADS), 0.1),
        'ret_norm_g': 1.0 + nrm(ks[11], (DEPTH, RET_W), 0.02),
        'attn_sink': nrm(ks[12], (DEPTH, ATT_Q_HEADS), 0.5),
        'cm_norm_g': 1.0 + nrm(ks[13], (DEPTH, CM_W), 0.02),
        'cm_w_s': nrm(ks[14], (DEPTH, CM_GROUPS, CHUNK, CHUNK), CHUNK ** -0.5),
        'cm_b_s': 1.0 + nrm(ks[15], (DEPTH, CM_GROUPS, CHUNK), 0.1),
        'w_out': nrm(ks[16], (DEPTH, MIX_W, D_MODEL), MIX_W ** -0.5),
        'w_gate': nrm(ks[17], (DEPTH, D_MODEL, FFN_HID), D_MODEL ** -0.5),
        'w_up': nrm(ks[18], (DEPTH, D_MODEL, FFN_HID), D_MODEL ** -0.5),
        'w_down': nrm(ks[19], (DEPTH, FFN_HID, D_MODEL), FFN_HID ** -0.5),
        'final_norm_g': 1.0 + nrm(ks[20], (D_MODEL,), 0.02),
    }


def reference(x, c, ctx, c_ctx, w_mod, b_mod, norm1_g, norm2_g, w_in, ret_decay_f, ret_decay_b,
              ret_norm_g, attn_sink, cm_norm_g, cm_w_s, cm_b_s, w_out, w_gate, w_up, w_down, final_norm_g):
    bsz, n, _ = x.shape
    m = ctx.shape[1]
    rope = axial_rope_tables(n)
    silu_c = jax.nn.silu(c)[:, None, :]
    silu_cc = jax.nn.silu(c_ctx)
    zero_state = jnp.zeros((bsz, RET_HEADS, HEAD_DIM, HEAD_DIM), F32)
    h = ctx
    for l in range(DEPTH):
        last = l == DEPTH - 1
        mx = jnp.split(silu_c @ w_mod[l] + b_mod[l], 6, axis=-1)
        mc = jnp.split(silu_cc @ w_mod[l] + b_mod[l], 6, axis=-1)

        zx = modulate(rms_norm(x, norm1_g[l]), mx[0], mx[1])
        zc = modulate(rms_norm(h, norm1_g[l]), mc[0], mc[1])
        rq, rk, rv, rg, aq, ak, av, cu, cv = split_in(zx @ w_in[l])
        crq, crk, crv, crg, caq, cak, cav, ccu, ccv = split_in(zc @ w_in[l])

        ret_c, st_f, st_b = retention_mixer(crq, crk, crv, crg, ret_decay_f[l], ret_decay_b[l], ret_norm_g[l],
                                            zero_state, zero_state, None)
        ret_x, _, _ = retention_mixer(rq, rk, rv, rg, ret_decay_f[l], ret_decay_b[l], ret_norm_g[l],
                                      st_f, st_b, rope)

        ck = cak.reshape(bsz, m, ATT_KV_HEADS, HEAD_DIM)
        cvv = cav.reshape(bsz, m, ATT_KV_HEADS, HEAD_DIM)
        q = apply_rope(aq.reshape(bsz, n, ATT_Q_HEADS, HEAD_DIM), *rope)
        k = apply_rope(ak.reshape(bsz, n, ATT_KV_HEADS, HEAD_DIM), *rope)
        att_x = window_attention(q, k, av.reshape(bsz, n, ATT_KV_HEADS, HEAD_DIM), ck, cvv, attn_sink[l])

        cm_x = chunk_gating(cu, cv, cm_norm_g[l], cm_w_s[l], cm_b_s[l])

        x = x + mx[2] * (jnp.concatenate([ret_x, att_x, cm_x], axis=-1) @ w_out[l])
        x = x + mx[5] * swiglu(modulate(rms_norm(x, norm2_g[l]), mx[3], mx[4]), w_gate[l], w_up[l], w_down[l])

        if not last:
            att_c = context_attention(caq.reshape(bsz, m, ATT_Q_HEADS, HEAD_DIM), ck, cvv, attn_sink[l])
            cm_c = chunk_gating(ccu, ccv, cm_norm_g[l], cm_w_s[l], cm_b_s[l])
            h = h + mc[2] * (jnp.concatenate([ret_c, att_c, cm_c], axis=-1) @ w_out[l])
            h = h + mc[5] * swiglu(modulate(rms_norm(h, norm2_g[l]), mc[3], mc[4]), w_gate[l], w_up[l], w_down[l])
    return rms_norm(x, final_norm_g)
```

```python
import functools

import jax
import jax.numpy as jnp
from jax import lax
from jax.experimental import pallas as pl
from jax.experimental.pallas import tpu as pltpu

F32 = jnp.float32
BF16 = jnp.bfloat16

LANES = 128
HEAD_DIM = 64
RET_HEADS = 4
RET_W = RET_HEADS * HEAD_DIM
ATT_Q_HEADS = 8
ATT_KV_HEADS = 2
ATT_GROUP = ATT_Q_HEADS // ATT_KV_HEADS
ATT_W = ATT_Q_HEADS * HEAD_DIM
KV_W = ATT_KV_HEADS * HEAD_DIM
CM_GROUPS = 4
CM_W = CM_GROUPS * HEAD_DIM
MIX_W = RET_W + ATT_W + CM_W
IN_W = 4 * RET_W + ATT_W + 2 * KV_W + 2 * CM_W
CHUNK = 128
GRID_W = 64
ROPE_BASE = 10000.0
AX_PAIRS = HEAD_DIM // 4
EPS = 1e-6
NEG = -1e30

OFF_RQ, OFF_RK, OFF_RV, OFF_RG = 0, RET_W, 2 * RET_W, 3 * RET_W
OFF_AQ = 4 * RET_W
OFF_AK = OFF_AQ + ATT_W
OFF_AV = OFF_AK + KV_W
OFF_CU = OFF_AV + KV_W
OFF_CV = OFF_CU + CM_W

VMEM_LIMIT = 56 * 1024 * 1024


def _params(sem):
    return pltpu.CompilerParams(dimension_semantics=sem, vmem_limit_bytes=VMEM_LIMIT)


def _silu(x):
    return x * (1.0 / (1.0 + jnp.exp(-x)))


def _gelu_tanh(x):
    return 0.5 * x * (1.0 + jnp.tanh(0.7978845608028654 * (x + 0.044715 * (x * x * x))))


def _lo_half_mask(shape):
    return (lax.broadcasted_iota(jnp.int32, shape, len(shape) - 1) % LANES) < HEAD_DIM


def _group_norm64(t, lo):
    zero = jnp.zeros_like(t)
    s_lo = jnp.sum(jnp.where(lo, t, zero), axis=-1, keepdims=True)
    s_hi = jnp.sum(jnp.where(lo, zero, t), axis=-1, keepdims=True)
    mu = jnp.where(lo, s_lo, s_hi) * (1.0 / HEAD_DIM)
    d = t - mu
    d2 = d * d
    v_lo = jnp.sum(jnp.where(lo, d2, zero), axis=-1, keepdims=True)
    v_hi = jnp.sum(jnp.where(lo, zero, d2), axis=-1, keepdims=True)
    var = jnp.where(lo, v_lo, v_hi) * (1.0 / HEAD_DIM)
    return d * lax.rsqrt(var + EPS)


def _mod_kernel(c_ref, w_ref, b_ref, o_ref):
    s = _silu(c_ref[...]).astype(BF16)
    o_ref[0] = jnp.dot(s, w_ref[0].astype(BF16), preferred_element_type=F32) + b_ref[0]


def _modulation(cin, w_mod, b_mod):
    depth, d, w6 = w_mod.shape
    tn = 768
    return pl.pallas_call(
        _mod_kernel,
        grid=(depth, w6 // tn),
        in_specs=[pl.BlockSpec((8, d), lambda l, j: (0, 0)),
                  pl.BlockSpec((1, d, tn), lambda l, j: (l, 0, j)),
                  pl.BlockSpec((1, 1, tn), lambda l, j: (l, 0, j))],
        out_specs=pl.BlockSpec((1, 8, tn), lambda l, j: (l, 0, j)),
        out_shape=jax.ShapeDtypeStruct((depth, 8, w6), F32),
        compiler_params=_params(("parallel", "parallel")),
        name="modulation",
    )(cin, w_mod, b_mod.reshape(depth, 1, w6))


def _rope_slab(t, cos, sin_up, sin_dn):
    return t * cos + pltpu.roll(t, LANES - AX_PAIRS, 1) * sin_up + pltpu.roll(t, AX_PAIRS, 1) * sin_dn


def _inproj_kernel(x_ref, mod_ref, g_ref, w_ref, cos_ref, sup_ref, sdn_ref, o_ref, *, rope):
    x = x_ref[0]
    y = x * lax.rsqrt(jnp.mean(x * x, axis=-1, keepdims=True) + EPS) * g_ref[...]
    z = (y * (1.0 + mod_ref[0, 1:2, :]) + mod_ref[0, 0:1, :]).astype(BF16)
    rope_cols = ((OFF_RQ, OFF_RV), (OFF_AQ, OFF_AV))
    scaled_cols = ((OFF_RK, OFF_RV), (OFF_AQ, OFF_AK))
    step = 2 * LANES
    for c0 in range(0, IN_W, step):
        acc = jnp.dot(z, w_ref[:, c0:c0 + step], preferred_element_type=F32)
        for s0 in range(c0, c0 + step, LANES):
            t = acc[:, s0 - c0:s0 - c0 + LANES]
            if rope and any(a <= s0 < b for a, b in rope_cols):
                t = _rope_slab(t, cos_ref[...], sup_ref[...], sdn_ref[...])
            if any(a <= s0 < b for a, b in scaled_cols):
                t = t * (HEAD_DIM ** -0.5)
            o_ref[0, :, s0:s0 + LANES] = t


def _inproj(x, mod_l, stream_of, g, w, rope_tabs, *, rope, tm):
    b, n, d = x.shape
    cos, sup, sdn = rope_tabs
    return pl.pallas_call(
        functools.partial(_inproj_kernel, rope=rope),
        grid=(b, n // tm),
        in_specs=[pl.BlockSpec((1, tm, d), lambda bi, i: (bi, i, 0)),
                  pl.BlockSpec((1, 6, d), lambda bi, i: (stream_of(bi), 0, 0)),
                  pl.BlockSpec((1, d), lambda bi, i: (0, 0)),
                  pl.BlockSpec((d, IN_W), lambda bi, i: (0, 0)),
                  pl.BlockSpec((tm, LANES), lambda bi, i: (i, 0)),
                  pl.BlockSpec((tm, LANES), lambda bi, i: (i, 0)),
                  pl.BlockSpec((tm, LANES), lambda bi, i: (i, 0))],
        out_specs=pl.BlockSpec((1, tm, IN_W), lambda bi, i: (bi, i, 0)),
        out_shape=jax.ShapeDtypeStruct((b, n, IN_W), F32),
        compiler_params=_params(("parallel", "parallel")),
        name="inproj_rope" if rope else "inproj_ctx",
    )(x, mod_l, g.reshape(1, d), w, cos, sup, sdn)


def _state_kernel(kf_ref, vf_ref, kb_ref, vb_ref, kdf_ref, kdb_ref, cdf_ref, cdb_ref, s0f_ref, s0b_ref,
                  sf_ref, sb_ref, ff_ref, fb_ref, stf, stb):
    c = pl.program_id(1)
    nc = pl.num_programs(1)

    @pl.when(c == 0)
    def _():
        stf[...] = s0f_ref[0]
        stb[...] = s0b_ref[0]

    sf_ref[0, 0] = stf[...]
    sb_ref[0, 0] = stb[...]
    row = lax.broadcasted_iota(jnp.int32, (LANES, LANES), 0)
    col = lax.broadcasted_iota(jnp.int32, (LANES, LANES), 1)
    same_head = (row < HEAD_DIM) == (col < HEAD_DIM)

    def update(st, k_ref, v_ref, kd_ref, cd_ref):
        kd = (k_ref[0] * kd_ref[...]).astype(BF16)
        v = v_ref[0].astype(BF16)
        for j in range(RET_W // LANES):
            sl = slice(j * LANES, (j + 1) * LANES)
            u = lax.dot_general(kd[:, sl], v[:, sl], (((0,), (0,)), ((), ())), preferred_element_type=F32)
            st[:, sl] = st[:, sl] * cd_ref[:, sl] + jnp.where(same_head, u, 0.0)

    update(stf, kf_ref, vf_ref, kdf_ref, cdf_ref)
    update(stb, kb_ref, vb_ref, kdb_ref, cdb_ref)

    @pl.when(c == nc - 1)
    def _():
        ff_ref[0] = stf[...]
        fb_ref[0] = stb[...]


def _ret_states(p, tabs, s0f, s0b):
    b, n, _ = p.shape
    nc = n // CHUNK
    kcol, vcol = OFF_RK // RET_W, OFF_RV // RET_W
    blk = (1, CHUNK, RET_W)
    st_spec = pl.BlockSpec((1, LANES, RET_W), lambda bi, c: (bi, 0, 0))
    tab = lambda r: pl.BlockSpec((r, RET_W), lambda bi, c: (0, 0))
    return pl.pallas_call(
        _state_kernel,
        grid=(b, nc),
        in_specs=[pl.BlockSpec(blk, lambda bi, c: (bi, c, kcol)),
                  pl.BlockSpec(blk, lambda bi, c: (bi, c, vcol)),
                  pl.BlockSpec(blk, lambda bi, c: (bi, nc - 1 - c, kcol)),
                  pl.BlockSpec(blk, lambda bi, c: (bi, nc - 1 - c, vcol)),
                  tab(CHUNK), tab(CHUNK), tab(1), tab(1), st_spec, st_spec],
        out_specs=[pl.BlockSpec((1, 1, LANES, RET_W), lambda bi, c: (bi, c, 0, 0)),
                   pl.BlockSpec((1, 1, LANES, RET_W), lambda bi, c: (bi, nc - 1 - c, 0, 0)),
                   st_spec, st_spec],
        out_shape=[jax.ShapeDtypeStruct((b, nc, LANES, RET_W), F32),
                   jax.ShapeDtypeStruct((b, nc, LANES, RET_W), F32),
                   jax.ShapeDtypeStruct((b, LANES, RET_W), F32),
                   jax.ShapeDtypeStruct((b, LANES, RET_W), F32)],
        scratch_shapes=[pltpu.VMEM((LANES, RET_W), F32), pltpu.VMEM((LANES, RET_W), F32)],
        compiler_params=_params(("parallel", "arbitrary")),
        name="ret_states",
    )(p, p, p, p, tabs["kdf"], tabs["kdb"], tabs["cdf"], tabs["cdb"], s0f, s0b)


def _mixer_kernel(*refs, local):
    if local:
        (rq_ref, rk_ref, rv_ref, rg_ref, aq_ref, cu_ref, cv_ref, ck_ref, cvv_ref, sf_ref, sb_ref,
         dm_ref, qdf_ref, qdb_ref, rng_ref, sink_ref, cng_ref, ws_ref, bs_ref,
         akm_ref, ak0_ref, akp_ref, avm_ref, av0_ref, avp_ref, o_ref) = refs
    else:
        (rq_ref, rk_ref, rv_ref, rg_ref, aq_ref, cu_ref, cv_ref, ck_ref, cvv_ref, sf_ref, sb_ref,
         dm_ref, qdf_ref, qdb_ref, rng_ref, sink_ref, cng_ref, ws_ref, bs_ref, o_ref) = refs
    c = pl.program_id(1)
    nc = pl.num_programs(1)
    lo = _lo_half_mask((CHUNK, LANES))
    contract_lanes = (((1,), (1,)), ((), ()))

    rq, rk, rv = rq_ref[0], rk_ref[0], rv_ref[0]
    qf = (rq * qdf_ref[...]).astype(BF16)
    qb = (rq * qdb_ref[...]).astype(BF16)
    for j in range(RET_W // LANES):
        sl = slice(j * LANES, (j + 1) * LANES)
        qp, kp, vp = rq[:, sl], rk[:, sl].astype(BF16), rv[:, sl].astype(BF16)
        heads = []
        for hl in range(2):
            qm = jnp.where(lo if hl == 0 else ~lo, qp, 0.0).astype(BF16)
            sc = lax.dot_general(qm, kp, contract_lanes, preferred_element_type=F32) * dm_ref[2 * j + hl]
            heads.append(jnp.dot(sc.astype(BF16), vp, preferred_element_type=F32))
        o = jnp.where(lo, heads[0], heads[1])
        o = o + jnp.dot(qf[:, sl], sf_ref[0, 0, :, sl].astype(BF16), preferred_element_type=F32)
        o = o + jnp.dot(qb[:, sl], sb_ref[0, 0, :, sl].astype(BF16), preferred_element_type=F32)
        o = _group_norm64(o, lo) * rng_ref[:, sl]
        o_ref[0, :, sl] = (_silu(rg_ref[0, :, sl]) * o).astype(o_ref.dtype)

    if local:
        keys = jnp.concatenate([akm_ref[0], ak0_ref[0], akp_ref[0], ck_ref[0]], axis=0)
        vals = jnp.concatenate([avm_ref[0], av0_ref[0], avp_ref[0], cvv_ref[0]], axis=0)
        n_loc = 3 * CHUNK
        shape = (ATT_GROUP * CHUNK, keys.shape[0])
        qi = lax.broadcasted_iota(jnp.int32, shape, 0) & (CHUNK - 1)
        kcol = lax.broadcasted_iota(jnp.int32, shape, 1)
        kpos = kcol - CHUNK
        first_key = jnp.where(c > 0, -CHUNK, 0)
        end_key = jnp.where(c < nc - 1, 2 * CHUNK, CHUNK)
        valid = (kcol >= n_loc) | ((jnp.abs(kpos - qi) <= CHUNK) & (kpos >= first_key) & (kpos < end_key))
    else:
        keys, vals = ck_ref[0], cvv_ref[0]
    lo_k = _lo_half_mask(keys.shape)
    keys_sw, vals_sw = pltpu.roll(keys, HEAD_DIM, 1), pltpu.roll(vals, HEAD_DIM, 1)
    for hk in range(ATT_KV_HEADS):
        first = lo_k if hk == 0 else ~lo_k
        kd = jnp.where(first, keys, keys_sw).astype(BF16)
        vd = jnp.where(first, vals, vals_sw).astype(BF16)
        qs, sinks = [], []
        for g in range(ATT_GROUP):
            h = hk * ATT_GROUP + g
            slab = aq_ref[0, :, (h // 2) * LANES:(h // 2 + 1) * LANES]
            qs.append(jnp.where(lo if h % 2 == 0 else ~lo, slab, 0.0))
            sinks.append(jnp.full((CHUNK, 1), sink_ref[0, h], F32))
        q = jnp.concatenate(qs, axis=0).astype(BF16)
        sink = jnp.concatenate(sinks, axis=0)
        s = lax.dot_general(q, kd, contract_lanes, preferred_element_type=F32)
        if local:
            s = jnp.where(valid, s, NEG)
        m = jnp.maximum(jnp.max(s, axis=-1, keepdims=True), sink)
        e = jnp.exp(s - m)
        denom = jnp.sum(e, axis=-1, keepdims=True) + jnp.exp(sink - m)
        pv = jnp.dot(e.astype(BF16), vd, preferred_element_type=F32) * (1.0 / denom)
        for t in range(ATT_GROUP // 2):
            even = pv[(2 * t) * CHUNK:(2 * t + 1) * CHUNK]
            odd = pv[(2 * t + 1) * CHUNK:(2 * t + 2) * CHUNK]
            col = RET_W + (hk * ATT_GROUP // 2 + t) * LANES
            o_ref[0, :, col:col + LANES] = jnp.where(lo, even, odd).astype(o_ref.dtype)

    for j in range(CM_W // LANES):
        sl = slice(j * LANES, (j + 1) * LANES)
        u = _gelu_tanh(cu_ref[0, :, sl])
        vn = (_group_norm64(_gelu_tanh(cv_ref[0, :, sl]), lo) * cng_ref[:, sl]).astype(BF16)
        s_even = jnp.dot(ws_ref[2 * j], vn, preferred_element_type=F32)
        s_odd = jnp.dot(ws_ref[2 * j + 1], vn, preferred_element_type=F32)
        sp = jnp.where(lo, s_even, s_odd) + bs_ref[:, sl]
        col = RET_W + ATT_W + j * LANES
        o_ref[0, :, col:col + LANES] = (u * sp).astype(o_ref.dtype)


def _mixer(p, pc, sf, sb, tabs, rng, sink, cng, ws, bs, *, local):
    b, n, _ = p.shape
    m = pc.shape[1]
    nc = n // CHUNK
    col = lambda off, w: off // w
    cur = lambda off, w: pl.BlockSpec((1, CHUNK, w), lambda bi, c: (bi, c, col(off, w)))
    full2 = lambda r, w: pl.BlockSpec((r, w), lambda bi, c: (0, 0))
    in_specs = [cur(OFF_RQ, RET_W), cur(OFF_RK, RET_W), cur(OFF_RV, RET_W), cur(OFF_RG, RET_W),
                cur(OFF_AQ, ATT_W), cur(OFF_CU, CM_W), cur(OFF_CV, CM_W),
                pl.BlockSpec((1, m, KV_W), lambda bi, c: (bi, 0, col(OFF_AK, KV_W))),
                pl.BlockSpec((1, m, KV_W), lambda bi, c: (bi, 0, col(OFF_AV, KV_W))),
                pl.BlockSpec((1, 1, LANES, RET_W), lambda bi, c: (bi, c, 0, 0)),
                pl.BlockSpec((1, 1, LANES, RET_W), lambda bi, c: (bi, c, 0, 0)),
                pl.BlockSpec((RET_HEADS, CHUNK, CHUNK), lambda bi, c: (0, 0, 0)),
                full2(CHUNK, RET_W), full2(CHUNK, RET_W), full2(1, RET_W),
                pl.BlockSpec(memory_space=pltpu.SMEM),
                full2(1, CM_W),
                pl.BlockSpec((CM_GROUPS, CHUNK, CHUNK), lambda bi, c: (0, 0, 0)),
                full2(CHUNK, CM_W)]
    args = [p, p, p, p, p, p, p, pc, pc, sf, sb, tabs["dm"], tabs["qdf"], tabs["qdb"], rng, sink, cng, ws, bs]
    if local:
        for off in (OFF_AK, OFF_AV):
            ci = col(off, KV_W)
            in_specs += [pl.BlockSpec((1, CHUNK, KV_W), lambda bi, c, ci=ci: (bi, jnp.maximum(c - 1, 0), ci)),
                         pl.BlockSpec((1, CHUNK, KV_W), lambda bi, c, ci=ci: (bi, c, ci)),
                         pl.BlockSpec((1, CHUNK, KV_W), lambda bi, c, ci=ci: (bi, jnp.minimum(c + 1, nc - 1), ci))]
            args += [p, p, p]
    return pl.pallas_call(
        functools.partial(_mixer_kernel, local=local),
        grid=(b, nc),
        in_specs=in_specs,
        out_specs=pl.BlockSpec((1, CHUNK, MIX_W), lambda bi, c: (bi, c, 0)),
        out_shape=jax.ShapeDtypeStruct((b, n, MIX_W), BF16),
        compiler_params=_params(("parallel", "parallel")),
        name="mixer_local" if local else "mixer_ctx",
    )(*args)


def _outffn_kernel(x_ref, mix_ref, mod_ref, g_ref, wo_ref, wg_ref, wu_ref, wd_ref, fg_ref, o_ref, *, th, final):
    a = jnp.dot(mix_ref[0], wo_ref[...], preferred_element_type=F32)
    x1 = x_ref[0] + mod_ref[0, 2:3, :] * a
    y = x1 * lax.rsqrt(jnp.mean(x1 * x1, axis=-1, keepdims=True) + EPS) * g_ref[...]
    z = (y * (1.0 + mod_ref[0, 4:5, :]) + mod_ref[0, 3:4, :]).astype(BF16)
    acc = jnp.zeros(x1.shape, F32)
    for h0 in range(0, wg_ref.shape[1], th):
        hg = jnp.dot(z, wg_ref[:, h0:h0 + th], preferred_element_type=F32)
        hu = jnp.dot(z, wu_ref[:, h0:h0 + th], preferred_element_type=F32)
        acc = acc + jnp.dot((_silu(hg) * hu).astype(BF16), wd_ref[h0:h0 + th, :], preferred_element_type=F32)
    x2 = x1 + mod_ref[0, 5:6, :] * acc
    if final:
        x2 = x2 * lax.rsqrt(jnp.mean(x2 * x2, axis=-1, keepdims=True) + EPS) * fg_ref[...]
    o_ref[0] = x2


def _outffn(x, mix, mod_l, stream_of, g, wo, wg, wu, wd, fg, *, tm, final):
    b, n, d = x.shape
    hid = wg.shape[1]
    const = lambda shape: pl.BlockSpec(shape, lambda bi, i: (0,) * len(shape), pipeline_mode=pl.Buffered(1))
    return pl.pallas_call(
        functools.partial(_outffn_kernel, th=256, final=final),
        grid=(b, n // tm),
        in_specs=[pl.BlockSpec((1, tm, d), lambda bi, i: (bi, i, 0)),
                  pl.BlockSpec((1, tm, MIX_W), lambda bi, i: (bi, i, 0)),
                  pl.BlockSpec((1, 6, d), lambda bi, i: (stream_of(bi), 0, 0)),
                  const((1, d)), const((MIX_W, d)), const((d, hid)), const((d, hid)), const((hid, d)),
                  const((1, d))],
        out_specs=pl.BlockSpec((1, tm, d), lambda bi, i: (bi, i, 0)),
        out_shape=jax.ShapeDtypeStruct((b, n, d), F32),
        compiler_params=_params(("parallel", "parallel")),
        name="outffn_final" if final else "outffn",
    )(x, mix, mod_l, g.reshape(1, d), wo, wg, wu, wd, fg.reshape(1, d))


def _rope_tables(n):
    pos = jnp.arange(n)
    row = (pos // GRID_W).astype(F32)
    colp = (pos % GRID_W).astype(F32)
    inv = 1.0 / (ROPE_BASE ** (jnp.arange(AX_PAIRS, dtype=F32) / AX_PAIRS))
    d = jnp.arange(LANES)
    ang = jnp.where(((d % HEAD_DIM) // (2 * AX_PAIRS) == 0)[None, :], row[:, None], colp[:, None]) * inv[d % AX_PAIRS][None, :]
    first = ((d % (2 * AX_PAIRS)) < AX_PAIRS)[None, :]
    cos, sin = jnp.cos(ang), jnp.sin(ang)
    return cos, jnp.where(first, -sin, 0.0), jnp.where(first, 0.0, sin)


def _decay_tables(decay_f, decay_b):
    lg_f = jax.nn.log_sigmoid(decay_f.astype(F32))
    lg_b = jax.nn.log_sigmoid(decay_b.astype(F32))
    idx = jnp.arange(CHUNK, dtype=F32)
    diff = idx[:, None] - idx[None, :]
    intra = lambda lg, dd: jnp.where(dd >= 0, jnp.exp(lg[:, None, None] * jnp.maximum(dd, 0.0)), 0.0)
    wide = lambda t: jnp.repeat(t.T, HEAD_DIM, axis=1)
    return {
        "dm": intra(lg_f, diff) + intra(lg_b, -diff),
        "qdf": wide(jnp.exp(lg_f[:, None] * (idx + 1.0))),
        "qdb": wide(jnp.exp(lg_b[:, None] * (CHUNK - idx))),
        "kdf": wide(jnp.exp(lg_f[:, None] * (CHUNK - 1.0 - idx))),
        "kdb": wide(jnp.exp(lg_b[:, None] * idx)),
        "cdf": wide(jnp.exp(lg_f[:, None] * CHUNK)),
        "cdb": wide(jnp.exp(lg_b[:, None] * CHUNK)),
    }


def kernel(x, c, ctx, c_ctx, w_mod, b_mod, norm1_g, norm2_g, w_in, ret_decay_f, ret_decay_b, ret_norm_g, attn_sink,
           cm_norm_g, cm_w_s, cm_b_s, w_out, w_gate, w_up, w_down, final_norm_g):
    bsz, n, d = x.shape
    m = ctx.shape[1]
    depth = w_in.shape[0]
    assert n % CHUNK == 0 and m % CHUNK == 0 and d % LANES == 0 and bsz + 1 <= 8
    tm = 512 if n % 512 == 0 else CHUNK

    cin = jnp.zeros((8, d), F32).at[:bsz].set(c).at[bsz].set(c_ctx)
    mod = _modulation(cin, w_mod, b_mod).reshape(depth, 8, 6, d)
    rope = _rope_tables(n)
    no_rope = tuple(jnp.zeros((m, LANES), F32) for _ in range(3))
    zero_state = jnp.zeros((bsz, LANES, RET_W), F32)
    latent_stream = lambda bi: bi
    ctx_stream = lambda bi: bsz

    h = ctx
    for l in range(depth):
        last = l == depth - 1
        tabs = _decay_tables(ret_decay_f[l], ret_decay_b[l])
        w_in_l = w_in[l].astype(BF16)
        wo, wg, wu, wd = (t[l].astype(BF16) for t in (w_out, w_gate, w_up, w_down))
        rng, cng = ret_norm_g[l].reshape(1, RET_W), cm_norm_g[l].reshape(1, CM_W)
        sink = attn_sink[l].reshape(1, ATT_Q_HEADS)
        ws = cm_w_s[l].astype(BF16)
        bs = jnp.repeat(cm_b_s[l].T, HEAD_DIM, axis=1)
        mix_args = (tabs, rng, sink, cng, ws, bs)

        pc = _inproj(h, mod[l], ctx_stream, norm1_g[l], w_in_l, no_rope, rope=False, tm=m)
        sfc, sbc, fin_f, fin_b = _ret_states(pc, tabs, zero_state, zero_state)
        p = _inproj(x, mod[l], latent_stream, norm1_g[l], w_in_l, rope, rope=True, tm=tm)
        sf, sb, _, _ = _ret_states(p, tabs, fin_f, fin_b)
        mix = _mixer(p, pc, sf, sb, *mix_args, local=True)
        x = _outffn(x, mix, mod[l], latent_stream, norm2_g[l], wo, wg, wu, wd, final_norm_g, tm=tm, final=last)
        if not last:
            mixc = _mixer(pc, pc, sfc, sbc, *mix_args, local=False)
            h = _outffn(h, mixc, mod[l], ctx_stream, norm2_g[l], wo, wg, wu, wd, final_norm_g, tm=m, final=False)
    return x
```

```python
import functools

import jax
import jax.numpy as jnp
from jax import lax
from jax.experimental import pallas as pl
from jax.experimental.pallas import tpu as pltpu

F32 = jnp.float32
BF16 = jnp.bfloat16

LANES = 128
HEAD_DIM = 64
RET_HEADS = 4
RET_W = RET_HEADS * HEAD_DIM
ATT_Q_HEADS = 8
ATT_KV_HEADS = 2
ATT_GROUP = ATT_Q_HEADS // ATT_KV_HEADS
ATT_W = ATT_Q_HEADS * HEAD_DIM
KV_W = ATT_KV_HEADS * HEAD_DIM
CM_GROUPS = 4
CM_W = CM_GROUPS * HEAD_DIM
MIX_W = RET_W + ATT_W + CM_W
IN_W = 4 * RET_W + ATT_W + 2 * KV_W + 2 * CM_W
CHUNK = 128
GRID_W = 64
ROPE_BASE = 10000.0
AX_PAIRS = HEAD_DIM // 4
EPS = 1e-6
NEG = -1e30
LOG2E = 1.4426950408889634

OFF_RQ, OFF_RK, OFF_RV, OFF_RG = 0, RET_W, 2 * RET_W, 3 * RET_W
OFF_AQ = 4 * RET_W
OFF_AK = OFF_AQ + ATT_W
OFF_AV = OFF_AK + KV_W
OFF_CU = OFF_AV + KV_W
OFF_CV = OFF_CU + CM_W

VMEM_LIMIT = 56 * 1024 * 1024


def _params(sem):
    return pltpu.CompilerParams(dimension_semantics=sem, vmem_limit_bytes=VMEM_LIMIT)


def _silu(x):
    return x * (1.0 / (1.0 + jnp.exp(-x)))


def _gelu_tanh(x):
    return 0.5 * x * (1.0 + jnp.tanh(0.7978845608028654 * (x + 0.044715 * (x * x * x))))


def _lo_half_mask(shape):
    return (lax.broadcasted_iota(jnp.int32, shape, len(shape) - 1) % LANES) < HEAD_DIM


def _group_norm64(t, lo):
    zero = jnp.zeros_like(t)
    s_lo = jnp.sum(jnp.where(lo, t, zero), axis=-1, keepdims=True)
    s_hi = jnp.sum(jnp.where(lo, zero, t), axis=-1, keepdims=True)
    mu = jnp.where(lo, s_lo, s_hi) * (1.0 / HEAD_DIM)
    d = t - mu
    d2 = d * d
    v_lo = jnp.sum(jnp.where(lo, d2, zero), axis=-1, keepdims=True)
    v_hi = jnp.sum(jnp.where(lo, zero, d2), axis=-1, keepdims=True)
    var = jnp.where(lo, v_lo, v_hi) * (1.0 / HEAD_DIM)
    return d * lax.rsqrt(var + EPS)


def _mod_kernel(c_ref, w_ref, b_ref, o_ref):
    s = _silu(c_ref[...]).astype(BF16)
    o_ref[0] = jnp.dot(s, w_ref[0].astype(BF16), preferred_element_type=F32) + b_ref[0]


def _modulation(cin, w_mod, b_mod):
    depth, d, w6 = w_mod.shape
    tn = 768
    return pl.pallas_call(
        _mod_kernel,
        grid=(depth, w6 // tn),
        in_specs=[pl.BlockSpec((8, d), lambda l, j: (0, 0)),
                  pl.BlockSpec((1, d, tn), lambda l, j: (l, 0, j)),
                  pl.BlockSpec((1, 1, tn), lambda l, j: (l, 0, j))],
        out_specs=pl.BlockSpec((1, 8, tn), lambda l, j: (l, 0, j)),
        out_shape=jax.ShapeDtypeStruct((depth, 8, w6), F32),
        compiler_params=_params(("parallel", "parallel")),
        name="modulation",
    )(cin, w_mod, b_mod.reshape(depth, 1, w6))


def _rope_slab(t, cos, sin_up, sin_dn):
    return t * cos + pltpu.roll(t, LANES - AX_PAIRS, 1) * sin_up + pltpu.roll(t, AX_PAIRS, 1) * sin_dn


def _col_scale(s0):
    if OFF_RK <= s0 < OFF_RV:
        return HEAD_DIM ** -0.5
    if OFF_AQ <= s0 < OFF_AK:
        return HEAD_DIM ** -0.5 * LOG2E
    return None


def _inproj_kernel(x_ref, mod_ref, g_ref, w_ref, cos_ref, sup_ref, sdn_ref, kdf_ref, kdb_ref,
                   o_ref, uf_ref, ub_ref, *, rope):
    x = x_ref[0]
    y = x * lax.rsqrt(jnp.mean(x * x, axis=-1, keepdims=True) + EPS) * g_ref[...]
    z = (y * (1.0 + mod_ref[0, 1:2, :]) + mod_ref[0, 0:1, :]).astype(BF16)
    rope_cols = ((OFF_RQ, OFF_RV), (OFF_AQ, OFF_AV))
    step = 2 * LANES
    for c0 in range(0, IN_W, step):
        acc = jnp.dot(z, w_ref[:, c0:c0 + step], preferred_element_type=F32)
        for s0 in range(c0, c0 + step, LANES):
            t = acc[:, s0 - c0:s0 - c0 + LANES]
            if rope and any(a <= s0 < b for a, b in rope_cols):
                t = _rope_slab(t, cos_ref[...], sup_ref[...], sdn_ref[...])
            if _col_scale(s0) is not None:
                t = t * _col_scale(s0)
            o_ref[0, :, s0:s0 + LANES] = t

    row = lax.broadcasted_iota(jnp.int32, (LANES, LANES), 0)
    col = lax.broadcasted_iota(jnp.int32, (LANES, LANES), 1)
    same_head = (row < HEAD_DIM) == (col < HEAD_DIM)
    for r in range(x.shape[0] // CHUNK):
        rows = slice(r * CHUNK, (r + 1) * CHUNK)
        k = o_ref[0, rows, OFF_RK:OFF_RK + RET_W]
        v = o_ref[0, rows, OFF_RV:OFF_RV + RET_W].astype(BF16)
        for kd_ref, u_ref in ((kdf_ref, uf_ref), (kdb_ref, ub_ref)):
            kd = (k * kd_ref[...]).astype(BF16)
            for j in range(RET_W // LANES):
                sl = slice(j * LANES, (j + 1) * LANES)
                u = lax.dot_general(kd[:, sl], v[:, sl], (((0,), (0,)), ((), ())), preferred_element_type=F32)
                u_ref[0, r, :, sl] = jnp.where(same_head, u, 0.0)


def _inproj(x, mod_l, stream_of, g, w, rope_tabs, tabs, *, rope, tm):
    b, n, d = x.shape
    cos, sup, sdn = rope_tabs
    nc, cpt = n // CHUNK, tm // CHUNK
    u_spec = pl.BlockSpec((1, cpt, LANES, RET_W), lambda bi, i: (bi, i, 0, 0))
    u_shape = jax.ShapeDtypeStruct((b, nc, LANES, RET_W), F32)
    return pl.pallas_call(
        functools.partial(_inproj_kernel, rope=rope),
        grid=(b, n // tm),
        in_specs=[pl.BlockSpec((1, tm, d), lambda bi, i: (bi, i, 0)),
                  pl.BlockSpec((1, 6, d), lambda bi, i: (stream_of(bi), 0, 0)),
                  pl.BlockSpec((1, d), lambda bi, i: (0, 0)),
                  pl.BlockSpec((d, IN_W), lambda bi, i: (0, 0)),
                  pl.BlockSpec((tm, LANES), lambda bi, i: (i, 0)),
                  pl.BlockSpec((tm, LANES), lambda bi, i: (i, 0)),
                  pl.BlockSpec((tm, LANES), lambda bi, i: (i, 0)),
                  pl.BlockSpec((CHUNK, RET_W), lambda bi, i: (0, 0)),
                  pl.BlockSpec((CHUNK, RET_W), lambda bi, i: (0, 0))],
        out_specs=[pl.BlockSpec((1, tm, IN_W), lambda bi, i: (bi, i, 0)), u_spec, u_spec],
        out_shape=[jax.ShapeDtypeStruct((b, n, IN_W), F32), u_shape, u_shape],
        compiler_params=_params(("parallel", "parallel")),
        name="inproj_rope" if rope else "inproj_ctx",
    )(x, mod_l, g.reshape(1, d), w, cos, sup, sdn, tabs["kdf"], tabs["kdb"])


def _scan_kernel(uf_ref, ub_ref, cdf_ref, cdb_ref, s0f_ref, s0b_ref, sf_ref, sb_ref, ff_ref, fb_ref, stf, stb, *, g):
    i = pl.program_id(1)

    @pl.when(i == 0)
    def _():
        stf[...] = s0f_ref[0]
        stb[...] = s0b_ref[0]

    st = stf[...]
    for t in range(g):
        sf_ref[0, t] = st.astype(sf_ref.dtype)
        st = st * cdf_ref[...] + uf_ref[0, t]
    stf[...] = st
    st = stb[...]
    for t in reversed(range(g)):
        sb_ref[0, t] = st.astype(sb_ref.dtype)
        st = st * cdb_ref[...] + ub_ref[0, t]
    stb[...] = st

    @pl.when(i == pl.num_programs(1) - 1)
    def _():
        ff_ref[0] = stf[...]
        fb_ref[0] = stb[...]


def _ret_scan(uf, ub, tabs, s0f, s0b):
    b, nc = uf.shape[:2]
    g = max(t for t in range(1, 17) if nc % t == 0)
    ns = nc // g
    blk = (1, g, LANES, RET_W)
    st_spec = pl.BlockSpec((1, LANES, RET_W), lambda bi, i: (bi, 0, 0))
    tab = pl.BlockSpec((1, RET_W), lambda bi, i: (0, 0))
    return pl.pallas_call(
        functools.partial(_scan_kernel, g=g),
        grid=(b, ns),
        in_specs=[pl.BlockSpec(blk, lambda bi, i: (bi, i, 0, 0)),
                  pl.BlockSpec(blk, lambda bi, i: (bi, ns - 1 - i, 0, 0)),
                  tab, tab, st_spec, st_spec],
        out_specs=[pl.BlockSpec(blk, lambda bi, i: (bi, i, 0, 0)),
                   pl.BlockSpec(blk, lambda bi, i: (bi, ns - 1 - i, 0, 0)),
                   st_spec, st_spec],
        out_shape=[jax.ShapeDtypeStruct((b, nc, LANES, RET_W), BF16),
                   jax.ShapeDtypeStruct((b, nc, LANES, RET_W), BF16),
                   jax.ShapeDtypeStruct((b, LANES, RET_W), F32),
                   jax.ShapeDtypeStruct((b, LANES, RET_W), F32)],
        scratch_shapes=[pltpu.VMEM((LANES, RET_W), F32), pltpu.VMEM((LANES, RET_W), F32)],
        compiler_params=_params(("parallel", "arbitrary")),
        name="ret_scan",
    )(uf, ub, tabs["cdf"], tabs["cdb"], s0f, s0b)


def _mixer_kernel(*refs, local, r):
    if local:
        (p_ref, ckv_ref, sf_ref, sb_ref, dm_ref, qdf_ref, qdb_ref, rng_ref, sink_ref, cng_ref, ws_ref, bs_ref,
         kvm_ref, kvp_ref, o_ref) = refs
    else:
        (p_ref, ckv_ref, sf_ref, sb_ref, dm_ref, qdf_ref, qdb_ref, rng_ref, sink_ref, cng_ref, ws_ref, bs_ref,
         o_ref) = refs
    step = pl.program_id(1)
    nsteps = pl.num_programs(1)
    lo = _lo_half_mask((CHUNK, LANES))
    contract_lanes = (((1,), (1,)), ((), ()))
    lane = lax.broadcasted_iota(jnp.int32, (CHUNK, RET_W), 1)
    head_of_lane = [(lane >= h * HEAD_DIM) & (lane < (h + 1) * HEAD_DIM) for h in range(RET_HEADS)]
    first_slab = lane < LANES
    if local:
        qi = lax.broadcasted_iota(jnp.int32, (CHUNK, CHUNK), 0)
        kj = lax.broadcasted_iota(jnp.int32, (CHUNK, CHUNK), 1)
    ctx_kv = ckv_ref[0]

    def per_head_rows(t):
        return jnp.concatenate([jnp.where(hm, t, 0.0) for hm in head_of_lane], axis=0).astype(BF16)

    def state_rows(s_ref, t):
        s = s_ref[0, t].astype(F32)
        return jnp.concatenate([jnp.where(first_slab, s, 0.0), jnp.where(first_slab, 0.0, s)], axis=0).astype(BF16)

    for t in range(r):
        rows = slice(t * CHUNK, (t + 1) * CHUNK)
        cols = lambda off, w: p_ref[0, rows, off:off + w]

        rq, rk, rv = cols(OFF_RQ, RET_W), cols(OFF_RK, RET_W), cols(OFF_RV, RET_W)
        sc = lax.dot_general(rq.astype(BF16), per_head_rows(rk), contract_lanes, preferred_element_type=F32)
        sc = sc * dm_ref[...]
        lhs = jnp.concatenate([sc.astype(BF16), (rq * qdf_ref[...]).astype(BF16), (rq * qdb_ref[...]).astype(BF16)],
                              axis=1)
        rhs = jnp.concatenate([per_head_rows(rv), state_rows(sf_ref, t), state_rows(sb_ref, t)], axis=0)
        o = jnp.dot(lhs, rhs, preferred_element_type=F32)
        for j in range(RET_W // LANES):
            sl = slice(j * LANES, (j + 1) * LANES)
            oj = _group_norm64(o[:, sl], lo) * rng_ref[:, sl]
            o_ref[0, rows, sl] = (_silu(cols(OFF_RG + j * LANES, LANES)) * oj).astype(o_ref.dtype)

        if local:
            kv_of = lambda u: p_ref[0, u * CHUNK:(u + 1) * CHUNK, OFF_AK:OFF_AK + 2 * KV_W]
            kv_prev = kvm_ref[0] if t == 0 else kv_of(t - 1)
            kv_next = kvp_ref[0] if t == r - 1 else kv_of(t + 1)
            kv = jnp.concatenate([kv_prev, kv_of(t), kv_next, ctx_kv], axis=0)
        else:
            kv = ctx_kv
        nk = kv.shape[0]
        keys = kv[:, :KV_W].astype(BF16)
        lo_k = _lo_half_mask((nk, LANES))
        vals = kv[:, KV_W:]
        vals0 = jnp.where(lo_k, vals, 1.0).astype(BF16)
        vals1 = jnp.where(lo_k, 1.0, vals).astype(BF16)
        qs, sinks = [], []
        for h in range(ATT_Q_HEADS):
            slab = cols(OFF_AQ + (h % ATT_GROUP) * LANES, LANES)
            qs.append(jnp.where(lo, slab, 0.0) if h < ATT_GROUP else jnp.where(lo, 0.0, slab))
            sinks.append(jnp.full((CHUNK, LANES), sink_ref[0, h] * LOG2E, F32))
        q = jnp.concatenate(qs, axis=0).astype(BF16)
        sink = jnp.concatenate(sinks, axis=0)
        half = q.shape[0] // 2
        s = jnp.concatenate([lax.dot_general(q[:half], keys, contract_lanes, preferred_element_type=F32),
                             lax.dot_general(q[half:], keys, contract_lanes, preferred_element_type=F32)], axis=0)
        if local:
            has_prev = jnp.where(step > 0, 0.0, NEG) if t == 0 else 0.0
            has_next = jnp.where(step < nsteps - 1, 0.0, NEG) if t == r - 1 else 0.0
            bias_prev = jnp.concatenate([jnp.where(kj >= qi, has_prev, NEG)] * ATT_Q_HEADS, axis=0)
            bias_next = jnp.concatenate([jnp.where(kj <= qi, has_next, NEG)] * ATT_Q_HEADS, axis=0)
            s = jnp.concatenate([s[:, :CHUNK] + bias_prev, s[:, CHUNK:2 * CHUNK],
                                 s[:, 2 * CHUNK:3 * CHUNK] + bias_next, s[:, 3 * CHUNK:]], axis=1)
        m = jnp.maximum(jnp.broadcast_to(jnp.max(s, axis=-1, keepdims=True), sink.shape), sink)
        e = jnp.exp2(s - jnp.concatenate([m] * (nk // LANES), axis=1)).astype(BF16)
        esink = jnp.exp2(sink - m)
        pv0 = jnp.dot(e[:half], vals0, preferred_element_type=F32)
        pv1 = jnp.dot(e[half:], vals1, preferred_element_type=F32)
        for u in range(ATT_GROUP):
            ru = slice(u * CHUNK, (u + 1) * CHUNK)
            a, b = pv0[ru], pv1[ru]
            num = jnp.where(lo, a, b)
            den = pltpu.roll(jnp.where(lo, b, a), HEAD_DIM, 1) + jnp.where(lo, esink[ru], esink[half:][ru])
            col = RET_W + u * LANES
            o_ref[0, rows, col:col + LANES] = (num * (1.0 / den)).astype(o_ref.dtype)

        u_act = _gelu_tanh(cols(OFF_CU, CM_W))
        vg = _gelu_tanh(cols(OFF_CV, CM_W))
        vn = jnp.concatenate([_group_norm64(vg[:, j * LANES:(j + 1) * LANES], lo) for j in range(CM_W // LANES)],
                             axis=1) * cng_ref[...]
        sp = jnp.dot(ws_ref[...], per_head_rows(vn), preferred_element_type=F32) + bs_ref[...]
        o_ref[0, rows, RET_W + ATT_W:] = (u_act * sp).astype(o_ref.dtype)


def _mixer(p, pc, sf, sb, tabs, rng, sink, cng, ws, bs, *, local, r):
    b, n, _ = p.shape
    m = pc.shape[1]
    ns = n // (r * CHUNK)
    kv_col = OFF_AK // (2 * KV_W)
    last_chunk = n // CHUNK - 1
    full2 = lambda rr, w: pl.BlockSpec((rr, w), lambda bi, c: (0, 0))
    in_specs = [pl.BlockSpec((1, r * CHUNK, IN_W), lambda bi, c: (bi, c, 0)),
                pl.BlockSpec((1, m, 2 * KV_W), lambda bi, c: (bi, 0, kv_col)),
                pl.BlockSpec((1, r, LANES, RET_W), lambda bi, c: (bi, c, 0, 0)),
                pl.BlockSpec((1, r, LANES, RET_W), lambda bi, c: (bi, c, 0, 0)),
                full2(CHUNK, RET_HEADS * CHUNK),
                full2(CHUNK, RET_W), full2(CHUNK, RET_W), full2(1, RET_W),
                pl.BlockSpec(memory_space=pltpu.SMEM),
                full2(1, CM_W),
                full2(CHUNK, CM_GROUPS * CHUNK),
                full2(CHUNK, CM_W)]
    args = [p, pc, sf, sb, tabs["dm"], tabs["qdf"], tabs["qdb"], rng, sink, cng, ws, bs]
    if local:
        in_specs += [pl.BlockSpec((1, CHUNK, 2 * KV_W), lambda bi, c: (bi, jnp.maximum(c * r - 1, 0), kv_col)),
                     pl.BlockSpec((1, CHUNK, 2 * KV_W), lambda bi, c: (bi, jnp.minimum((c + 1) * r, last_chunk), kv_col))]
        args += [p, p]
    return pl.pallas_call(
        functools.partial(_mixer_kernel, local=local, r=r),
        grid=(b, ns),
        in_specs=in_specs,
        out_specs=pl.BlockSpec((1, r * CHUNK, MIX_W), lambda bi, c: (bi, c, 0)),
        out_shape=jax.ShapeDtypeStruct((b, n, MIX_W), BF16),
        compiler_params=_params(("parallel", "parallel")),
        name="mixer_local" if local else "mixer_ctx",
    )(*args)


def _outffn_kernel(x_ref, mix_ref, mod_ref, g_ref, wo_ref, wg_ref, wu_ref, wd_ref, fg_ref, o_ref, *, th, final):
    a = jnp.dot(mix_ref[0], wo_ref[...], preferred_element_type=F32)
    x1 = x_ref[0] + mod_ref[0, 2:3, :] * a
    y = x1 * lax.rsqrt(jnp.mean(x1 * x1, axis=-1, keepdims=True) + EPS) * g_ref[...]
    z = (y * (1.0 + mod_ref[0, 4:5, :]) + mod_ref[0, 3:4, :]).astype(BF16)
    acc = jnp.zeros(x1.shape, F32)
    for h0 in range(0, wg_ref.shape[1], th):
        hg = jnp.dot(z, wg_ref[:, h0:h0 + th], preferred_element_type=F32)
        hu = jnp.dot(z, wu_ref[:, h0:h0 + th], preferred_element_type=F32)
        acc = acc + jnp.dot((_silu(hg) * hu).astype(BF16), wd_ref[h0:h0 + th, :], preferred_element_type=F32)
    x2 = x1 + mod_ref[0, 5:6, :] * acc
    if final:
        x2 = x2 * lax.rsqrt(jnp.mean(x2 * x2, axis=-1, keepdims=True) + EPS) * fg_ref[...]
    o_ref[0] = x2


def _outffn(x, mix, mod_l, stream_of, g, wo, wg, wu, wd, fg, *, tm, final):
    b, n, d = x.shape
    hid = wg.shape[1]
    const = lambda shape: pl.BlockSpec(shape, lambda bi, i: (0,) * len(shape), pipeline_mode=pl.Buffered(1))
    return pl.pallas_call(
        functools.partial(_outffn_kernel, th=256, final=final),
        grid=(b, n // tm),
        in_specs=[pl.BlockSpec((1, tm, d), lambda bi, i: (bi, i, 0)),
                  pl.BlockSpec((1, tm, MIX_W), lambda bi, i: (bi, i, 0)),
                  pl.BlockSpec((1, 6, d), lambda bi, i: (stream_of(bi), 0, 0)),
                  const((1, d)), const((MIX_W, d)), const((d, hid)), const((d, hid)), const((hid, d)),
                  const((1, d))],
        out_specs=pl.BlockSpec((1, tm, d), lambda bi, i: (bi, i, 0)),
        out_shape=jax.ShapeDtypeStruct((b, n, d), F32),
        compiler_params=_params(("parallel", "parallel")),
        name="outffn_final" if final else "outffn",
    )(x, mix, mod_l, g.reshape(1, d), wo, wg, wu, wd, fg.reshape(1, d))


def _rope_tables(n):
    pos = jnp.arange(n)
    row = (pos // GRID_W).astype(F32)
    colp = (pos % GRID_W).astype(F32)
    inv = 1.0 / (ROPE_BASE ** (jnp.arange(AX_PAIRS, dtype=F32) / AX_PAIRS))
    d = jnp.arange(LANES)
    ang = jnp.where(((d % HEAD_DIM) // (2 * AX_PAIRS) == 0)[None, :], row[:, None], colp[:, None]) * inv[d % AX_PAIRS][None, :]
    first = ((d % (2 * AX_PAIRS)) < AX_PAIRS)[None, :]
    cos, sin = jnp.cos(ang), jnp.sin(ang)
    return cos, jnp.where(first, -sin, 0.0), jnp.where(first, 0.0, sin)


def _decay_tables(decay_f, decay_b):
    lg_f = jax.nn.log_sigmoid(decay_f.astype(F32))
    lg_b = jax.nn.log_sigmoid(decay_b.astype(F32))
    idx = jnp.arange(CHUNK, dtype=F32)
    diff = idx[:, None] - idx[None, :]
    intra = lambda lg, dd: jnp.where(dd >= 0, jnp.exp(lg[:, None, None] * jnp.maximum(dd, 0.0)), 0.0)
    wide = lambda t: jnp.repeat(t.T, HEAD_DIM, axis=1)
    return {
        "dm": (intra(lg_f, diff) + intra(lg_b, -diff)).transpose(1, 0, 2).reshape(CHUNK, RET_HEADS * CHUNK),
        "qdf": wide(jnp.exp(lg_f[:, None] * (idx + 1.0))),
        "qdb": wide(jnp.exp(lg_b[:, None] * (CHUNK - idx))),
        "kdf": wide(jnp.exp(lg_f[:, None] * (CHUNK - 1.0 - idx))),
        "kdb": wide(jnp.exp(lg_b[:, None] * idx)),
        "cdf": wide(jnp.exp(lg_f[:, None] * CHUNK)),
        "cdb": wide(jnp.exp(lg_b[:, None] * CHUNK)),
    }


def kernel(x, c, ctx, c_ctx, w_mod, b_mod, norm1_g, norm2_g, w_in, ret_decay_f, ret_decay_b, ret_norm_g, attn_sink,
           cm_norm_g, cm_w_s, cm_b_s, w_out, w_gate, w_up, w_down, final_norm_g):
    bsz, n, d = x.shape
    m = ctx.shape[1]
    depth = w_in.shape[0]
    assert n % CHUNK == 0 and m % CHUNK == 0 and d % LANES == 0 and bsz + 1 <= 8
    tm = 512 if n % 512 == 0 else CHUNK

    cin = jnp.zeros((8, d), F32).at[:bsz].set(c).at[bsz].set(c_ctx)
    mod = _modulation(cin, w_mod, b_mod).reshape(depth, 8, 6, d)
    rope = _rope_tables(n)
    no_rope = tuple(jnp.zeros((m, LANES), F32) for _ in range(3))
    zero_state = jnp.zeros((bsz, LANES, RET_W), F32)
    latent_stream = lambda bi: bi
    ctx_stream = lambda bi: bsz

    head_order = jnp.arange(ATT_Q_HEADS).reshape(ATT_KV_HEADS, ATT_GROUP).T.reshape(-1)
    att_perm = (head_order[:, None] * HEAD_DIM + jnp.arange(HEAD_DIM)[None, :]).reshape(-1)
    in_cols = jnp.arange(IN_W).at[OFF_AQ:OFF_AK].set(OFF_AQ + att_perm)
    mix_rows = jnp.arange(MIX_W).at[RET_W:RET_W + ATT_W].set(RET_W + att_perm)

    h = ctx
    for l in range(depth):
        last = l == depth - 1
        tabs = _decay_tables(ret_decay_f[l], ret_decay_b[l])
        w_in_l = w_in[l][:, in_cols].astype(BF16)
        wo = w_out[l][mix_rows].astype(BF16)
        wg, wu, wd = (t[l].astype(BF16) for t in (w_gate, w_up, w_down))
        rng, cng = ret_norm_g[l].reshape(1, RET_W), cm_norm_g[l].reshape(1, CM_W)
        sink = attn_sink[l].reshape(1, ATT_Q_HEADS)
        ws = cm_w_s[l].transpose(1, 0, 2).reshape(CHUNK, CM_GROUPS * CHUNK).astype(BF16)
        bs = jnp.repeat(cm_b_s[l].T, HEAD_DIM, axis=1)
        mix_args = (tabs, rng, sink, cng, ws, bs)

        pc, ufc, ubc = _inproj(h, mod[l], ctx_stream, norm1_g[l], w_in_l, no_rope, tabs, rope=False, tm=m)
        sfc, sbc, fin_f, fin_b = _ret_scan(ufc, ubc, tabs, zero_state, zero_state)
        p, uf, ub = _inproj(x, mod[l], latent_stream, norm1_g[l], w_in_l, rope, tabs, rope=True, tm=tm)
        sf, sb, _, _ = _ret_scan(uf, ub, tabs, fin_f, fin_b)
        mix = _mixer(p, pc, sf, sb, *mix_args, local=True, r=2 if n % (2 * CHUNK) == 0 else 1)
        x = _outffn(x, mix, mod[l], latent_stream, norm2_g[l], wo, wg, wu, wd, final_norm_g, tm=tm, final=last)
        if not last:
            mixc = _mixer(pc, pc, sfc, sbc, *mix_args, local=False, r=m // CHUNK)
            h = _outffn(h, mixc, mod[l], ctx_stream, norm2_g[l], wo, wg, wu, wd, final_norm_g, tm=m, final=False)
    return x
```

```python
import functools

import jax
import jax.numpy as jnp
from jax import lax
from jax.experimental import pallas as pl
from jax.experimental.pallas import tpu as pltpu

F32 = jnp.float32
BF16 = jnp.bfloat16

LANES = 128
HEAD_DIM = 64
RET_HEADS = 4
RET_W = RET_HEADS * HEAD_DIM
ATT_Q_HEADS = 8
ATT_KV_HEADS = 2
ATT_GROUP = ATT_Q_HEADS // ATT_KV_HEADS
ATT_W = ATT_Q_HEADS * HEAD_DIM
KV_W = ATT_KV_HEADS * HEAD_DIM
CM_GROUPS = 4
CM_W = CM_GROUPS * HEAD_DIM
MIX_W = RET_W + ATT_W + CM_W
IN_W = 4 * RET_W + ATT_W + 2 * KV_W + 2 * CM_W
CHUNK = 128
GRID_W = 64
ROPE_BASE = 10000.0
AX_PAIRS = HEAD_DIM // 4
EPS = 1e-6
NEG = -1e30
LOG2E = 1.4426950408889634

OFF_RQ, OFF_RK, OFF_RV, OFF_RG = 0, RET_W, 2 * RET_W, 3 * RET_W
OFF_AQ = 4 * RET_W
OFF_AK = OFF_AQ + ATT_W
OFF_AV = OFF_AK + KV_W
OFF_CU = OFF_AV + KV_W
OFF_CV = OFF_CU + CM_W

VMEM_LIMIT = 56 * 1024 * 1024


def _params(sem):
    return pltpu.CompilerParams(dimension_semantics=sem, vmem_limit_bytes=VMEM_LIMIT)


def _silu(x):
    return x * (1.0 / (1.0 + jnp.exp(-x)))


def _gelu_tanh(x):
    return 0.5 * x * (1.0 + jnp.tanh(0.7978845608028654 * (x + 0.044715 * (x * x * x))))


def _lo_half_mask(shape):
    return (lax.broadcasted_iota(jnp.int32, shape, len(shape) - 1) % LANES) < HEAD_DIM


def _group_norm64(t, lo):
    zero = jnp.zeros_like(t)
    s_lo = jnp.sum(jnp.where(lo, t, zero), axis=-1, keepdims=True)
    s_hi = jnp.sum(jnp.where(lo, zero, t), axis=-1, keepdims=True)
    mu = jnp.where(lo, s_lo, s_hi) * (1.0 / HEAD_DIM)
    d = t - mu
    d2 = d * d
    v_lo = jnp.sum(jnp.where(lo, d2, zero), axis=-1, keepdims=True)
    v_hi = jnp.sum(jnp.where(lo, zero, d2), axis=-1, keepdims=True)
    var = jnp.where(lo, v_lo, v_hi) * (1.0 / HEAD_DIM)
    return d * lax.rsqrt(var + EPS)


def _mod_kernel(c_ref, w_ref, b_ref, o_ref):
    s = _silu(c_ref[...]).astype(BF16)
    o_ref[0] = jnp.dot(s, w_ref[0].astype(BF16), preferred_element_type=F32) + b_ref[0]


def _modulation(cin, w_mod, b_mod):
    depth, d, w6 = w_mod.shape
    tn = 768
    return pl.pallas_call(
        _mod_kernel,
        grid=(depth, w6 // tn),
        in_specs=[pl.BlockSpec((8, d), lambda l, j: (0, 0)),
                  pl.BlockSpec((1, d, tn), lambda l, j: (l, 0, j)),
                  pl.BlockSpec((1, 1, tn), lambda l, j: (l, 0, j))],
        out_specs=pl.BlockSpec((1, 8, tn), lambda l, j: (l, 0, j)),
        out_shape=jax.ShapeDtypeStruct((depth, 8, w6), F32),
        compiler_params=_params(("parallel", "parallel")),
        name="modulation",
    )(cin, w_mod, b_mod.reshape(depth, 1, w6))


def _rope_slab(t, cos, sin_up, sin_dn):
    return t * cos + pltpu.roll(t, LANES - AX_PAIRS, 1) * sin_up + pltpu.roll(t, AX_PAIRS, 1) * sin_dn


def _col_scale(s0):
    if OFF_RK <= s0 < OFF_RV:
        return HEAD_DIM ** -0.5
    if OFF_AQ <= s0 < OFF_AK:
        return HEAD_DIM ** -0.5 * LOG2E
    return None


def _inproj_kernel(x_ref, mod_ref, g_ref, w_ref, cos_ref, sup_ref, sdn_ref, kdf_ref, kdb_ref,
                   o_ref, uf_ref, ub_ref, *, rope):
    x = x_ref[0]
    y = x * lax.rsqrt(jnp.mean(x * x, axis=-1, keepdims=True) + EPS) * g_ref[...]
    z = (y * (1.0 + mod_ref[0, 1:2, :]) + mod_ref[0, 0:1, :]).astype(BF16)
    rope_cols = ((OFF_RQ, OFF_RV), (OFF_AQ, OFF_AV))
    step = 2 * LANES
    for c0 in range(0, IN_W, step):
        acc = jnp.dot(z, w_ref[:, c0:c0 + step], preferred_element_type=F32)
        for s0 in range(c0, c0 + step, LANES):
            t = acc[:, s0 - c0:s0 - c0 + LANES]
            if rope and any(a <= s0 < b for a, b in rope_cols):
                t = _rope_slab(t, cos_ref[...], sup_ref[...], sdn_ref[...])
            if _col_scale(s0) is not None:
                t = t * _col_scale(s0)
            o_ref[0, :, s0:s0 + LANES] = t

    lo = _lo_half_mask((HEAD_DIM, LANES))
    for r in range(x.shape[0] // CHUNK):
        rows = slice(r * CHUNK, (r + 1) * CHUNK)
        k = o_ref[0, rows, OFF_RK:OFF_RK + RET_W]
        v = o_ref[0, rows, OFF_RV:OFF_RV + RET_W].astype(BF16)
        for kd_ref, u_ref in ((kdf_ref, uf_ref), (kdb_ref, ub_ref)):
            kd = (k * kd_ref[...]).astype(BF16)
            for j in range(RET_W // LANES):
                sl = slice(j * LANES, (j + 1) * LANES)
                u = lax.dot_general(kd[:, sl], v[:, sl], (((0,), (0,)), ((), ())), preferred_element_type=F32)
                u_ref[0, r, :, sl] = jnp.where(lo, u[:HEAD_DIM], u[HEAD_DIM:])


def _inproj(x, l, mod, stream_of, g, w, rope_tabs, tabs, *, rope, tm):
    b, n, d = x.shape
    layer = lambda *shape: pl.BlockSpec((None,) + shape, lambda bi, i: (l,) + (0,) * len(shape))
    cos, sup, sdn = rope_tabs
    nc, cpt = n // CHUNK, tm // CHUNK
    u_spec = pl.BlockSpec((1, cpt, HEAD_DIM, RET_W), lambda bi, i: (bi, i, 0, 0))
    u_shape = jax.ShapeDtypeStruct((b, nc, HEAD_DIM, RET_W), F32)
    return pl.pallas_call(
        functools.partial(_inproj_kernel, rope=rope),
        grid=(b, n // tm),
        in_specs=[pl.BlockSpec((1, tm, d), lambda bi, i: (bi, i, 0)),
                  pl.BlockSpec((None, 1, 6, d), lambda bi, i: (l, stream_of(bi), 0, 0)),
                  layer(1, d), layer(d, IN_W),
                  pl.BlockSpec((tm, LANES), lambda bi, i: (i, 0)),
                  pl.BlockSpec((tm, LANES), lambda bi, i: (i, 0)),
                  pl.BlockSpec((tm, LANES), lambda bi, i: (i, 0)),
                  layer(CHUNK, RET_W), layer(CHUNK, RET_W)],
        out_specs=[pl.BlockSpec((1, tm, IN_W), lambda bi, i: (bi, i, 0)), u_spec, u_spec],
        out_shape=[jax.ShapeDtypeStruct((b, n, IN_W), F32), u_shape, u_shape],
        compiler_params=_params(("parallel", "parallel")),
        name="inproj_rope" if rope else "inproj_ctx",
    )(x, mod, g, w, cos, sup, sdn, tabs["kdf"], tabs["kdb"])


def _scan_kernel(uf_ref, ub_ref, cdf_ref, cdb_ref, s0f_ref, s0b_ref, sf_ref, sb_ref, ff_ref, fb_ref, stf, stb, *, g):
    i = pl.program_id(1)

    @pl.when(i == 0)
    def _():
        stf[...] = s0f_ref[0]
        stb[...] = s0b_ref[0]

    st = stf[...]
    for t in range(g):
        sf_ref[0, t] = st.astype(sf_ref.dtype)
        st = st * cdf_ref[...] + uf_ref[0, t]
    stf[...] = st
    st = stb[...]
    for t in reversed(range(g)):
        sb_ref[0, t] = st.astype(sb_ref.dtype)
        st = st * cdb_ref[...] + ub_ref[0, t]
    stb[...] = st

    @pl.when(i == pl.num_programs(1) - 1)
    def _():
        ff_ref[0] = stf[...]
        fb_ref[0] = stb[...]


def _ret_scan(uf, ub, l, tabs, s0f, s0b):
    b, nc = uf.shape[:2]
    g = max(t for t in range(1, 17) if nc % t == 0)
    ns = nc // g
    blk = (1, g, HEAD_DIM, RET_W)
    st_spec = pl.BlockSpec((1, HEAD_DIM, RET_W), lambda bi, i: (bi, 0, 0))
    tab = pl.BlockSpec((None, 1, RET_W), lambda bi, i: (l, 0, 0))
    return pl.pallas_call(
        functools.partial(_scan_kernel, g=g),
        grid=(b, ns),
        in_specs=[pl.BlockSpec(blk, lambda bi, i: (bi, i, 0, 0)),
                  pl.BlockSpec(blk, lambda bi, i: (bi, ns - 1 - i, 0, 0)),
                  tab, tab, st_spec, st_spec],
        out_specs=[pl.BlockSpec(blk, lambda bi, i: (bi, i, 0, 0)),
                   pl.BlockSpec(blk, lambda bi, i: (bi, ns - 1 - i, 0, 0)),
                   st_spec, st_spec],
        out_shape=[jax.ShapeDtypeStruct((b, nc, HEAD_DIM, RET_W), BF16),
                   jax.ShapeDtypeStruct((b, nc, HEAD_DIM, RET_W), BF16),
                   jax.ShapeDtypeStruct((b, HEAD_DIM, RET_W), F32),
                   jax.ShapeDtypeStruct((b, HEAD_DIM, RET_W), F32)],
        scratch_shapes=[pltpu.VMEM((HEAD_DIM, RET_W), F32), pltpu.VMEM((HEAD_DIM, RET_W), F32)],
        compiler_params=_params(("parallel", "arbitrary")),
        name="ret_scan",
    )(uf, ub, tabs["cdf"], tabs["cdb"], s0f, s0b)


def _mixer_kernel(*refs, local, r, l):
    if local:
        (p_ref, ckv_ref, sf_ref, sb_ref, dm_ref, qdf_ref, qdb_ref, rng_ref, sink_ref, cng_ref, ws_ref, bs_ref,
         kvm_ref, kvp_ref, o_ref) = refs
    else:
        (p_ref, ckv_ref, sf_ref, sb_ref, dm_ref, qdf_ref, qdb_ref, rng_ref, sink_ref, cng_ref, ws_ref, bs_ref,
         o_ref) = refs
    step = pl.program_id(1)
    nsteps = pl.num_programs(1)
    lo = _lo_half_mask((CHUNK, LANES))
    contract_lanes = (((1,), (1,)), ((), ()))
    lane = lax.broadcasted_iota(jnp.int32, (CHUNK, RET_W), 1)
    head_of_lane = [(lane >= h * HEAD_DIM) & (lane < (h + 1) * HEAD_DIM) for h in range(RET_HEADS)]
    lane_s = lax.broadcasted_iota(jnp.int32, (HEAD_DIM, RET_W), 1)
    head_of_state_lane = [(lane_s >= h * HEAD_DIM) & (lane_s < (h + 1) * HEAD_DIM) for h in range(RET_HEADS)]
    if local:
        qi = lax.broadcasted_iota(jnp.int32, (CHUNK, CHUNK), 0)
        kj = lax.broadcasted_iota(jnp.int32, (CHUNK, CHUNK), 1)
    ctx_kv = ckv_ref[0]

    def per_head_rows(t):
        return jnp.concatenate([jnp.where(hm, t, 0.0) for hm in head_of_lane], axis=0).astype(BF16)

    def state_rows(s_ref, t):
        s = s_ref[0, t].astype(F32)
        return jnp.concatenate([jnp.where(hm, s, 0.0) for hm in head_of_state_lane], axis=0).astype(BF16)

    for t in range(r):
        rows = slice(t * CHUNK, (t + 1) * CHUNK)
        cols = lambda off, w: p_ref[0, rows, off:off + w]

        rq, rk, rv = cols(OFF_RQ, RET_W), cols(OFF_RK, RET_W), cols(OFF_RV, RET_W)
        sc = lax.dot_general(rq.astype(BF16), per_head_rows(rk), contract_lanes, preferred_element_type=F32)
        sc = sc * dm_ref[...]
        lhs = jnp.concatenate([sc.astype(BF16), (rq * qdf_ref[...]).astype(BF16), (rq * qdb_ref[...]).astype(BF16)],
                              axis=1)
        rhs = jnp.concatenate([per_head_rows(rv), state_rows(sf_ref, t), state_rows(sb_ref, t)], axis=0)
        o = jnp.dot(lhs, rhs, preferred_element_type=F32)
        for j in range(RET_W // LANES):
            sl = slice(j * LANES, (j + 1) * LANES)
            oj = _group_norm64(o[:, sl], lo) * rng_ref[:, sl]
            o_ref[0, rows, sl] = (_silu(cols(OFF_RG + j * LANES, LANES)) * oj).astype(o_ref.dtype)

        if local:
            kv_of = lambda u: p_ref[0, u * CHUNK:(u + 1) * CHUNK, OFF_AK:OFF_AK + 2 * KV_W]
            kv_prev = kvm_ref[0] if t == 0 else kv_of(t - 1)
            kv_next = kvp_ref[0] if t == r - 1 else kv_of(t + 1)
            kv = jnp.concatenate([kv_prev, kv_of(t), kv_next, ctx_kv], axis=0)
        else:
            kv = ctx_kv
        nk = kv.shape[0]
        keys = kv[:, :KV_W].astype(BF16)
        lo_k = _lo_half_mask((nk, LANES))
        vals = kv[:, KV_W:]
        vals0 = jnp.where(lo_k, vals, 1.0).astype(BF16)
        vals1 = jnp.where(lo_k, 1.0, vals).astype(BF16)
        qs, sinks = [], []
        for h in range(ATT_Q_HEADS):
            slab = cols(OFF_AQ + (h % ATT_GROUP) * LANES, LANES)
            qs.append(jnp.where(lo, slab, 0.0) if h < ATT_GROUP else jnp.where(lo, 0.0, slab))
            sinks.append(jnp.full((CHUNK, LANES), sink_ref[l, h] * LOG2E, F32))
        q = jnp.concatenate(qs, axis=0).astype(BF16)
        sink = jnp.concatenate(sinks, axis=0)
        half = q.shape[0] // 2
        s = jnp.concatenate([lax.dot_general(q[:half], keys, contract_lanes, preferred_element_type=F32),
                             lax.dot_general(q[half:], keys, contract_lanes, preferred_element_type=F32)], axis=0)
        if local:
            has_prev = jnp.where(step > 0, 0.0, NEG) if t == 0 else 0.0
            has_next = jnp.where(step < nsteps - 1, 0.0, NEG) if t == r - 1 else 0.0
            bias_prev = jnp.concatenate([jnp.where(kj >= qi, has_prev, NEG)] * ATT_Q_HEADS, axis=0)
            bias_next = jnp.concatenate([jnp.where(kj <= qi, has_next, NEG)] * ATT_Q_HEADS, axis=0)
            s = jnp.concatenate([s[:, :CHUNK] + bias_prev, s[:, CHUNK:2 * CHUNK],
                                 s[:, 2 * CHUNK:3 * CHUNK] + bias_next, s[:, 3 * CHUNK:]], axis=1)
        m = jnp.maximum(jnp.broadcast_to(jnp.max(s, axis=-1, keepdims=True), sink.shape), sink)
        e = jnp.exp2(s - jnp.concatenate([m] * (nk // LANES), axis=1)).astype(BF16)
        esink = jnp.exp2(sink - m)
        pv0 = jnp.dot(e[:half], vals0, preferred_element_type=F32)
        pv1 = jnp.dot(e[half:], vals1, preferred_element_type=F32)
        for u in range(ATT_GROUP):
            ru = slice(u * CHUNK, (u + 1) * CHUNK)
            a, b = pv0[ru], pv1[ru]
            num = jnp.where(lo, a, b)
            den = pltpu.roll(jnp.where(lo, b, a), HEAD_DIM, 1) + jnp.where(lo, esink[ru], esink[half:][ru])
            col = RET_W + u * LANES
            o_ref[0, rows, col:col + LANES] = (num * (1.0 / den)).astype(o_ref.dtype)

        u_act = _gelu_tanh(cols(OFF_CU, CM_W))
        vg = _gelu_tanh(cols(OFF_CV, CM_W))
        vn = jnp.concatenate([_group_norm64(vg[:, j * LANES:(j + 1) * LANES], lo) for j in range(CM_W // LANES)],
                             axis=1) * cng_ref[...]
        sp = jnp.dot(ws_ref[...], per_head_rows(vn), preferred_element_type=F32) + bs_ref[...]
        o_ref[0, rows, RET_W + ATT_W:] = (u_act * sp).astype(o_ref.dtype)


def _mixer(p, pc, sf, sb, l, tabs, rng, sink, cng, ws, bs, *, local, r):
    b, n, _ = p.shape
    m = pc.shape[1]
    ns = n // (r * CHUNK)
    kv_col = OFF_AK // (2 * KV_W)
    last_chunk = n // CHUNK - 1
    layer = lambda rr, w: pl.BlockSpec((None, rr, w), lambda bi, c: (l, 0, 0))
    in_specs = [pl.BlockSpec((1, r * CHUNK, IN_W), lambda bi, c: (bi, c, 0)),
                pl.BlockSpec((1, m, 2 * KV_W), lambda bi, c: (bi, 0, kv_col)),
                pl.BlockSpec((1, r, HEAD_DIM, RET_W), lambda bi, c: (bi, c, 0, 0)),
                pl.BlockSpec((1, r, HEAD_DIM, RET_W), lambda bi, c: (bi, c, 0, 0)),
                layer(CHUNK, RET_HEADS * CHUNK),
                layer(CHUNK, RET_W), layer(CHUNK, RET_W), layer(1, RET_W),
                pl.BlockSpec(memory_space=pltpu.SMEM),
                layer(1, CM_W),
                layer(CHUNK, CM_GROUPS * CHUNK),
                layer(CHUNK, CM_W)]
    args = [p, pc, sf, sb, tabs["dm"], tabs["qdf"], tabs["qdb"], rng, sink, cng, ws, bs]
    if local:
        in_specs += [pl.BlockSpec((1, CHUNK, 2 * KV_W), lambda bi, c: (bi, jnp.maximum(c * r - 1, 0), kv_col)),
                     pl.BlockSpec((1, CHUNK, 2 * KV_W), lambda bi, c: (bi, jnp.minimum((c + 1) * r, last_chunk), kv_col))]
        args += [p, p]
    return pl.pallas_call(
        functools.partial(_mixer_kernel, local=local, r=r, l=l),
        grid=(b, ns),
        in_specs=in_specs,
        out_specs=pl.BlockSpec((1, r * CHUNK, MIX_W), lambda bi, c: (bi, c, 0)),
        out_shape=jax.ShapeDtypeStruct((b, n, MIX_W), BF16),
        compiler_params=_params(("parallel", "parallel")),
        name="mixer_local" if local else "mixer_ctx",
    )(*args)


def _outffn_kernel(x_ref, mix_ref, mod_ref, g_ref, wo_ref, wg_ref, wu_ref, wd_ref, fg_ref, o_ref, *, th, final):
    a = jnp.dot(mix_ref[0], wo_ref[...], preferred_element_type=F32)
    x1 = x_ref[0] + mod_ref[0, 2:3, :] * a
    y = x1 * lax.rsqrt(jnp.mean(x1 * x1, axis=-1, keepdims=True) + EPS) * g_ref[...]
    z = (y * (1.0 + mod_ref[0, 4:5, :]) + mod_ref[0, 3:4, :]).astype(BF16)
    acc = jnp.zeros(x1.shape, F32)
    for h0 in range(0, wg_ref.shape[1], th):
        hg = jnp.dot(z, wg_ref[:, h0:h0 + th], preferred_element_type=F32)
        hu = jnp.dot(z, wu_ref[:, h0:h0 + th], preferred_element_type=F32)
        acc = acc + jnp.dot((_silu(hg) * hu).astype(BF16), wd_ref[h0:h0 + th, :], preferred_element_type=F32)
    x2 = x1 + mod_ref[0, 5:6, :] * acc
    if final:
        x2 = x2 * lax.rsqrt(jnp.mean(x2 * x2, axis=-1, keepdims=True) + EPS) * fg_ref[...]
    o_ref[0] = x2


def _outffn(x, mix, l, mod, stream_of, g, wo, wg, wu, wd, fg, *, tm, final):
    b, n, d = x.shape
    hid = wg.shape[-1]
    layer = lambda *shape: pl.BlockSpec((None,) + shape, lambda bi, i: (l,) + (0,) * len(shape),
                                        pipeline_mode=pl.Buffered(1))
    return pl.pallas_call(
        functools.partial(_outffn_kernel, th=256, final=final),
        grid=(b, n // tm),
        in_specs=[pl.BlockSpec((1, tm, d), lambda bi, i: (bi, i, 0)),
                  pl.BlockSpec((1, tm, MIX_W), lambda bi, i: (bi, i, 0)),
                  pl.BlockSpec((None, 1, 6, d), lambda bi, i: (l, stream_of(bi), 0, 0)),
                  layer(1, d), layer(MIX_W, d), layer(d, hid), layer(d, hid), layer(hid, d),
                  pl.BlockSpec((1, d), lambda bi, i: (0, 0))],
        out_specs=pl.BlockSpec((1, tm, d), lambda bi, i: (bi, i, 0)),
        out_shape=jax.ShapeDtypeStruct((b, n, d), F32),
        compiler_params=_params(("parallel", "parallel")),
        name="outffn_final" if final else "outffn",
    )(x, mix, mod, g, wo, wg, wu, wd, fg.reshape(1, d))


def _rope_tables(n):
    rows = n // GRID_W
    inv = 1.0 / (ROPE_BASE ** (jnp.arange(AX_PAIRS, dtype=F32) / AX_PAIRS))
    grid = (rows, GRID_W, AX_PAIRS)
    ang_r = jnp.broadcast_to((jnp.arange(rows, dtype=F32)[:, None] * inv)[:, None, :], grid)
    ang_c = jnp.broadcast_to((jnp.arange(GRID_W, dtype=F32)[:, None] * inv)[None, :, :], grid)
    zero = jnp.zeros(grid, F32)
    lanes = lambda parts: jnp.concatenate(list(parts) * (LANES // HEAD_DIM), axis=-1).reshape(n, LANES)
    cos = lanes([jnp.cos(ang_r), jnp.cos(ang_r), jnp.cos(ang_c), jnp.cos(ang_c)])
    sin_up = lanes([-jnp.sin(ang_r), zero, -jnp.sin(ang_c), zero])
    sin_dn = lanes([zero, jnp.sin(ang_r), zero, jnp.sin(ang_c)])
    return cos, sin_up, sin_dn


def _decay_tables(decay_f, decay_b):
    lg_f = jax.nn.log_sigmoid(decay_f.astype(F32))
    lg_b = jax.nn.log_sigmoid(decay_b.astype(F32))
    depth = lg_f.shape[0]
    idx = jnp.arange(CHUNK, dtype=F32)
    diff = idx[:, None] - idx[None, :]
    intra = lambda lg, dd: jnp.where(dd >= 0, jnp.exp(lg[:, :, None, None] * jnp.maximum(dd, 0.0)), 0.0)
    wide = lambda t: jnp.repeat(jnp.swapaxes(t, 1, 2), HEAD_DIM, axis=2)
    return {
        "dm": jnp.swapaxes(intra(lg_f, diff) + intra(lg_b, -diff), 1, 2).reshape(depth, CHUNK, RET_HEADS * CHUNK),
        "qdf": wide(jnp.exp(lg_f[:, :, None] * (idx + 1.0))),
        "qdb": wide(jnp.exp(lg_b[:, :, None] * (CHUNK - idx))),
        "kdf": wide(jnp.exp(lg_f[:, :, None] * (CHUNK - 1.0 - idx))),
        "kdb": wide(jnp.exp(lg_b[:, :, None] * idx)),
        "cdf": wide(jnp.exp(lg_f[:, :, None] * CHUNK)),
        "cdb": wide(jnp.exp(lg_b[:, :, None] * CHUNK)),
    }


def kernel(x, c, ctx, c_ctx, w_mod, b_mod, norm1_g, norm2_g, w_in, ret_decay_f, ret_decay_b, ret_norm_g, attn_sink,
           cm_norm_g, cm_w_s, cm_b_s, w_out, w_gate, w_up, w_down, final_norm_g):
    bsz, n, d = x.shape
    m = ctx.shape[1]
    depth = w_in.shape[0]
    assert n % CHUNK == 0 and m % CHUNK == 0 and d % LANES == 0 and bsz + 1 <= 8
    tm = 512 if n % 512 == 0 else CHUNK

    cin = jnp.zeros((8, d), F32).at[:bsz].set(c).at[bsz].set(c_ctx)
    mod = _modulation(cin, w_mod, b_mod).reshape(depth, 8, 6, d)
    rope = _rope_tables(n)
    no_rope = tuple(jnp.zeros((m, LANES), F32) for _ in range(3))
    zero_state = jnp.zeros((bsz, HEAD_DIM, RET_W), F32)
    latent_stream = lambda bi: bi
    ctx_stream = lambda bi: bsz

    aq = w_in[:, :, OFF_AQ:OFF_AK].reshape(depth, d, ATT_KV_HEADS, ATT_GROUP, HEAD_DIM)
    aq = jnp.swapaxes(aq, 2, 3).reshape(depth, d, ATT_W)
    w_in_b = jnp.concatenate([w_in[:, :, :OFF_AQ], aq, w_in[:, :, OFF_AK:]], axis=2).astype(BF16)
    att = w_out[:, RET_W:RET_W + ATT_W].reshape(depth, ATT_KV_HEADS, ATT_GROUP, HEAD_DIM, d)
    att = jnp.swapaxes(att, 1, 2).reshape(depth, ATT_W, d)
    wo_b = jnp.concatenate([w_out[:, :RET_W], att, w_out[:, RET_W + ATT_W:]], axis=1).astype(BF16)
    wg_b, wu_b, wd_b = w_gate.astype(BF16), w_up.astype(BF16), w_down.astype(BF16)

    tabs = _decay_tables(ret_decay_f, ret_decay_b)
    g1, g2 = norm1_g.reshape(depth, 1, d), norm2_g.reshape(depth, 1, d)
    rng, cng = ret_norm_g.reshape(depth, 1, RET_W), cm_norm_g.reshape(depth, 1, CM_W)
    ws = jnp.swapaxes(cm_w_s, 1, 2).reshape(depth, CHUNK, CM_GROUPS * CHUNK).astype(BF16)
    bs = jnp.repeat(jnp.swapaxes(cm_b_s, 1, 2), HEAD_DIM, axis=2)
    mix_args = (tabs, rng, attn_sink, cng, ws, bs)
    ffn_w = (wo_b, wg_b, wu_b, wd_b, final_norm_g)

    h = ctx
    for l in range(depth):
        last = l == depth - 1
        pc, ufc, ubc = _inproj(h, l, mod, ctx_stream, g1, w_in_b, no_rope, tabs, rope=False, tm=m)
        sfc, sbc, fin_f, fin_b = _ret_scan(ufc, ubc, l, tabs, zero_state, zero_state)
        p, uf, ub = _inproj(x, l, mod, latent_stream, g1, w_in_b, rope, tabs, rope=True, tm=tm)
        sf, sb, _, _ = _ret_scan(uf, ub, l, tabs, fin_f, fin_b)
        mix = _mixer(p, pc, sf, sb, l, *mix_args, local=True, r=2 if n % (2 * CHUNK) == 0 else 1)
        x = _outffn(x, mix, l, mod, latent_stream, g2, *ffn_w, tm=tm, final=last)
        if not last:
            mixc = _mixer(pc, pc, sfc, sbc, l, *mix_args, local=False, r=m // CHUNK)
            h = _outffn(h, mixc, l, mod, ctx_stream, g2, *ffn_w, tm=m, final=False)
    return x
```

```python
import functools

import jax
import jax.numpy as jnp
from jax import lax
from jax.experimental import pallas as pl
from jax.experimental.pallas import tpu as pltpu

F32 = jnp.float32
BF16 = jnp.bfloat16

LANES = 128
HEAD_DIM = 64
RET_HEADS = 4
RET_W = RET_HEADS * HEAD_DIM
ATT_Q_HEADS = 8
ATT_KV_HEADS = 2
ATT_GROUP = ATT_Q_HEADS // ATT_KV_HEADS
ATT_W = ATT_Q_HEADS * HEAD_DIM
KV_W = ATT_KV_HEADS * HEAD_DIM
CM_GROUPS = 4
CM_W = CM_GROUPS * HEAD_DIM
MIX_W = RET_W + ATT_W + CM_W
IN_W = 4 * RET_W + ATT_W + 2 * KV_W + 2 * CM_W
CHUNK = 128
GRID_W = 64
ROPE_BASE = 10000.0
AX_PAIRS = HEAD_DIM // 4
EPS = 1e-6
NEG = -1e30
LOG2E = 1.4426950408889634

OFF_RQ, OFF_RK, OFF_RV, OFF_RG = 0, RET_W, 2 * RET_W, 3 * RET_W
OFF_AQ = 4 * RET_W
OFF_AK = OFF_AQ + ATT_W
OFF_AV = OFF_AK + KV_W
OFF_CU = OFF_AV + KV_W
OFF_CV = OFF_CU + CM_W

VMEM_LIMIT = 56 * 1024 * 1024


def _params(sem):
    return pltpu.CompilerParams(dimension_semantics=sem, vmem_limit_bytes=VMEM_LIMIT)


def _silu(x):
    return x * (1.0 / (1.0 + jnp.exp(-x)))


def _gelu_tanh(x):
    return 0.5 * x * (1.0 + jnp.tanh(0.7978845608028654 * (x + 0.044715 * (x * x * x))))


def _lo_half_mask(shape):
    return (lax.broadcasted_iota(jnp.int32, shape, len(shape) - 1) % LANES) < HEAD_DIM


def _group_norm64(t, lo):
    zero = jnp.zeros_like(t)
    s_lo = jnp.sum(jnp.where(lo, t, zero), axis=-1, keepdims=True)
    s_hi = jnp.sum(jnp.where(lo, zero, t), axis=-1, keepdims=True)
    mu = jnp.where(lo, s_lo, s_hi) * (1.0 / HEAD_DIM)
    d = t - mu
    d2 = d * d
    v_lo = jnp.sum(jnp.where(lo, d2, zero), axis=-1, keepdims=True)
    v_hi = jnp.sum(jnp.where(lo, zero, d2), axis=-1, keepdims=True)
    var = jnp.where(lo, v_lo, v_hi) * (1.0 / HEAD_DIM)
    return d * lax.rsqrt(var + EPS)


def _mod_kernel(c_ref, w_ref, b_ref, o_ref):
    s = _silu(c_ref[...]).astype(BF16)
    o_ref[0] = jnp.dot(s, w_ref[0].astype(BF16), preferred_element_type=F32) + b_ref[0]


def _modulation(cin, w_mod, b_mod):
    depth, d, w6 = w_mod.shape
    tn = 768
    return pl.pallas_call(
        _mod_kernel,
        grid=(depth, w6 // tn),
        in_specs=[pl.BlockSpec((8, d), lambda l, j: (0, 0)),
                  pl.BlockSpec((1, d, tn), lambda l, j: (l, 0, j)),
                  pl.BlockSpec((1, 1, tn), lambda l, j: (l, 0, j))],
        out_specs=pl.BlockSpec((1, 8, tn), lambda l, j: (l, 0, j)),
        out_shape=jax.ShapeDtypeStruct((depth, 8, w6), F32),
        compiler_params=_params(("parallel", "parallel")),
        name="modulation",
    )(cin, w_mod, b_mod.reshape(depth, 1, w6))


def _rope_slab(t, cos, sin_up, sin_dn):
    return t * cos + pltpu.roll(t, LANES - AX_PAIRS, 1) * sin_up + pltpu.roll(t, AX_PAIRS, 1) * sin_dn


def _col_scale(s0):
    if OFF_RK <= s0 < OFF_RV:
        return HEAD_DIM ** -0.5
    if OFF_AQ <= s0 < OFF_AK:
        return HEAD_DIM ** -0.5 * LOG2E
    return None


def _inproj_kernel(x_ref, mod_ref, g_ref, w_ref, cos_ref, sup_ref, sdn_ref, kdf_ref, kdb_ref,
                   o_ref, uf_ref, ub_ref, *, rope):
    x = x_ref[0]
    y = x * lax.rsqrt(jnp.mean(x * x, axis=-1, keepdims=True) + EPS) * g_ref[...]
    z = (y * (1.0 + mod_ref[0, 1:2, :]) + mod_ref[0, 0:1, :]).astype(BF16)
    rope_cols = ((OFF_RQ, OFF_RV), (OFF_AQ, OFF_AV))
    step = 2 * LANES
    for c0 in range(0, IN_W, step):
        acc = jnp.dot(z, w_ref[:, c0:c0 + step], preferred_element_type=F32)
        for s0 in range(c0, c0 + step, LANES):
            t = acc[:, s0 - c0:s0 - c0 + LANES]
            if rope and any(a <= s0 < b for a, b in rope_cols):
                t = _rope_slab(t, cos_ref[...], sup_ref[...], sdn_ref[...])
            if _col_scale(s0) is not None:
                t = t * _col_scale(s0)
            o_ref[0, :, s0:s0 + LANES] = t

    lo = _lo_half_mask((HEAD_DIM, LANES))
    for r in range(x.shape[0] // CHUNK):
        rows = slice(r * CHUNK, (r + 1) * CHUNK)
        k = o_ref[0, rows, OFF_RK:OFF_RK + RET_W]
        v = o_ref[0, rows, OFF_RV:OFF_RV + RET_W].astype(BF16)
        for kd_ref, u_ref in ((kdf_ref, uf_ref), (kdb_ref, ub_ref)):
            kd = (k * kd_ref[...]).astype(BF16)
            for j in range(RET_W // LANES):
                sl = slice(j * LANES, (j + 1) * LANES)
                u = lax.dot_general(kd[:, sl], v[:, sl], (((0,), (0,)), ((), ())), preferred_element_type=F32)
                u_ref[0, r, :, sl] = jnp.where(lo, u[:HEAD_DIM], u[HEAD_DIM:])


def _inproj(x, l, mod, stream_of, g, w, rope_tabs, tabs, *, rope, tm):
    b, n, d = x.shape
    layer = lambda *shape: pl.BlockSpec((None,) + shape, lambda bi, i: (l,) + (0,) * len(shape))
    cos, sup, sdn = rope_tabs
    nc, cpt = n // CHUNK, tm // CHUNK
    u_spec = pl.BlockSpec((1, cpt, HEAD_DIM, RET_W), lambda bi, i: (bi, i, 0, 0))
    u_shape = jax.ShapeDtypeStruct((b, nc, HEAD_DIM, RET_W), F32)
    return pl.pallas_call(
        functools.partial(_inproj_kernel, rope=rope),
        grid=(b, n // tm),
        in_specs=[pl.BlockSpec((1, tm, d), lambda bi, i: (bi, i, 0)),
                  pl.BlockSpec((None, 1, 6, d), lambda bi, i: (l, stream_of(bi), 0, 0)),
                  layer(1, d), layer(d, IN_W),
                  pl.BlockSpec((tm, LANES), lambda bi, i: (i, 0)),
                  pl.BlockSpec((tm, LANES), lambda bi, i: (i, 0)),
                  pl.BlockSpec((tm, LANES), lambda bi, i: (i, 0)),
                  layer(CHUNK, RET_W), layer(CHUNK, RET_W)],
        out_specs=[pl.BlockSpec((1, tm, IN_W), lambda bi, i: (bi, i, 0)), u_spec, u_spec],
        out_shape=[jax.ShapeDtypeStruct((b, n, IN_W), F32), u_shape, u_shape],
        compiler_params=_params(("parallel", "parallel")),
        name="inproj_rope" if rope else "inproj_ctx",
    )(x, mod, g, w, cos, sup, sdn, tabs["kdf"], tabs["kdb"])


def _scan_kernel(uf_ref, ub_ref, cdf_ref, cdb_ref, s0f_ref, s0b_ref, sf_ref, sb_ref, ff_ref, fb_ref, stf, stb, *, g):
    i = pl.program_id(1)

    @pl.when(i == 0)
    def _():
        stf[...] = s0f_ref[0]
        stb[...] = s0b_ref[0]

    st = stf[...]
    for t in range(g):
        sf_ref[0, t] = st.astype(sf_ref.dtype)
        st = st * cdf_ref[...] + uf_ref[0, t]
    stf[...] = st
    st = stb[...]
    for t in reversed(range(g)):
        sb_ref[0, t] = st.astype(sb_ref.dtype)
        st = st * cdb_ref[...] + ub_ref[0, t]
    stb[...] = st

    @pl.when(i == pl.num_programs(1) - 1)
    def _():
        ff_ref[0] = stf[...]
        fb_ref[0] = stb[...]


def _ret_scan(uf, ub, l, tabs, s0f, s0b):
    b, nc = uf.shape[:2]
    g = max(t for t in range(1, 17) if nc % t == 0)
    ns = nc // g
    blk = (1, g, HEAD_DIM, RET_W)
    st_spec = pl.BlockSpec((1, HEAD_DIM, RET_W), lambda bi, i: (bi, 0, 0))
    tab = pl.BlockSpec((None, 1, RET_W), lambda bi, i: (l, 0, 0))
    return pl.pallas_call(
        functools.partial(_scan_kernel, g=g),
        grid=(b, ns),
        in_specs=[pl.BlockSpec(blk, lambda bi, i: (bi, i, 0, 0)),
                  pl.BlockSpec(blk, lambda bi, i: (bi, ns - 1 - i, 0, 0)),
                  tab, tab, st_spec, st_spec],
        out_specs=[pl.BlockSpec(blk, lambda bi, i: (bi, i, 0, 0)),
                   pl.BlockSpec(blk, lambda bi, i: (bi, ns - 1 - i, 0, 0)),
                   st_spec, st_spec],
        out_shape=[jax.ShapeDtypeStruct((b, nc, HEAD_DIM, RET_W), BF16),
                   jax.ShapeDtypeStruct((b, nc, HEAD_DIM, RET_W), BF16),
                   jax.ShapeDtypeStruct((b, HEAD_DIM, RET_W), F32),
                   jax.ShapeDtypeStruct((b, HEAD_DIM, RET_W), F32)],
        scratch_shapes=[pltpu.VMEM((HEAD_DIM, RET_W), F32), pltpu.VMEM((HEAD_DIM, RET_W), F32)],
        compiler_params=_params(("parallel", "arbitrary")),
        name="ret_scan",
    )(uf, ub, tabs["cdf"], tabs["cdb"], s0f, s0b)


def _mixer_kernel(*refs, local, r, l):
    if local:
        (p_ref, ckv_ref, sf_ref, sb_ref, dm_ref, qdf_ref, qdb_ref, rng_ref, sink_ref, cng_ref, ws_ref, bs_ref,
         kvm_ref, kvp_ref, o_ref) = refs
    else:
        (p_ref, ckv_ref, sf_ref, sb_ref, dm_ref, qdf_ref, qdb_ref, rng_ref, sink_ref, cng_ref, ws_ref, bs_ref,
         o_ref) = refs
    step = pl.program_id(1)
    nsteps = pl.num_programs(1)
    lo = _lo_half_mask((CHUNK, LANES))
    contract_lanes = (((1,), (1,)), ((), ()))
    lane = lax.broadcasted_iota(jnp.int32, (CHUNK, RET_W), 1)
    head_of_lane = [(lane >= h * HEAD_DIM) & (lane < (h + 1) * HEAD_DIM) for h in range(RET_HEADS)]
    lane_s = lax.broadcasted_iota(jnp.int32, (HEAD_DIM, RET_W), 1)
    head_of_state_lane = [(lane_s >= h * HEAD_DIM) & (lane_s < (h + 1) * HEAD_DIM) for h in range(RET_HEADS)]
    if local:
        qi = lax.broadcasted_iota(jnp.int32, (CHUNK, CHUNK), 0)
        kj = lax.broadcasted_iota(jnp.int32, (CHUNK, CHUNK), 1)
    ctx_kv = ckv_ref[0]

    def per_head_rows(t):
        return jnp.concatenate([jnp.where(hm, t, 0.0) for hm in head_of_lane], axis=0).astype(BF16)

    def state_rows(s_ref, t):
        s = s_ref[0, t].astype(F32)
        return jnp.concatenate([jnp.where(hm, s, 0.0) for hm in head_of_state_lane], axis=0).astype(BF16)

    rows_of = lambda t: slice(t * CHUNK, (t + 1) * CHUNK)
    cols_of = lambda t: (lambda off, w: p_ref[0, rows_of(t), off:off + w])
    att = {}

    def retention(t):
        cols = cols_of(t)
        rq, rk, rv = cols(OFF_RQ, RET_W), cols(OFF_RK, RET_W), cols(OFF_RV, RET_W)
        sc = lax.dot_general(rq.astype(BF16), per_head_rows(rk), contract_lanes, preferred_element_type=F32)
        sc = sc * dm_ref[...]
        lhs = jnp.concatenate([sc.astype(BF16), (rq * qdf_ref[...]).astype(BF16), (rq * qdb_ref[...]).astype(BF16)],
                              axis=1)
        rhs = jnp.concatenate([per_head_rows(rv), state_rows(sf_ref, t), state_rows(sb_ref, t)], axis=0)
        o = jnp.dot(lhs, rhs, preferred_element_type=F32)
        for j in range(RET_W // LANES):
            sl = slice(j * LANES, (j + 1) * LANES)
            oj = _group_norm64(o[:, sl], lo) * rng_ref[:, sl]
            o_ref[0, rows_of(t), sl] = (_silu(cols(OFF_RG + j * LANES, LANES)) * oj).astype(o_ref.dtype)

    def attention_scores(t):
        cols = cols_of(t)
        if local:
            kv_of = lambda u: p_ref[0, rows_of(u), OFF_AK:OFF_AK + 2 * KV_W]
            kv_prev = kvm_ref[0] if t == 0 else kv_of(t - 1)
            kv_next = kvp_ref[0] if t == r - 1 else kv_of(t + 1)
            kv = jnp.concatenate([kv_prev, kv_of(t), kv_next, ctx_kv], axis=0)
        else:
            kv = ctx_kv
        nk = kv.shape[0]
        keys = kv[:, :KV_W].astype(BF16)
        lo_k = _lo_half_mask((nk, LANES))
        vals = kv[:, KV_W:]
        vals0 = jnp.where(lo_k, vals, 1.0).astype(BF16)
        vals1 = jnp.where(lo_k, 1.0, vals).astype(BF16)
        qs, sinks = [], []
        for h in range(ATT_Q_HEADS):
            slab = cols(OFF_AQ + (h % ATT_GROUP) * LANES, LANES)
            qs.append(jnp.where(lo, slab, 0.0) if h < ATT_GROUP else jnp.where(lo, 0.0, slab))
            sinks.append(jnp.full((CHUNK, LANES), sink_ref[l, h] * LOG2E, F32))
        q = jnp.concatenate(qs, axis=0).astype(BF16)
        half = q.shape[0] // 2
        s = jnp.concatenate([lax.dot_general(q[:half], keys, contract_lanes, preferred_element_type=F32),
                             lax.dot_general(q[half:], keys, contract_lanes, preferred_element_type=F32)], axis=0)
        att[t] = (s, jnp.concatenate(sinks, axis=0), vals0, vals1)

    def attention_softmax(t):
        s, sink, vals0, vals1 = att[t]
        if local:
            has_prev = jnp.where(step > 0, 0.0, NEG) if t == 0 else 0.0
            has_next = jnp.where(step < nsteps - 1, 0.0, NEG) if t == r - 1 else 0.0
            bias_prev = jnp.concatenate([jnp.where(kj >= qi, has_prev, NEG)] * ATT_Q_HEADS, axis=0)
            bias_next = jnp.concatenate([jnp.where(kj <= qi, has_next, NEG)] * ATT_Q_HEADS, axis=0)
            s = jnp.concatenate([s[:, :CHUNK] + bias_prev, s[:, CHUNK:2 * CHUNK],
                                 s[:, 2 * CHUNK:3 * CHUNK] + bias_next, s[:, 3 * CHUNK:]], axis=1)
        m = jnp.maximum(jnp.broadcast_to(jnp.max(s, axis=-1, keepdims=True), sink.shape), sink)
        e = jnp.exp2(s - jnp.concatenate([m] * (s.shape[1] // LANES), axis=1)).astype(BF16)
        att[t] = (e, jnp.exp2(sink - m), vals0, vals1)

    def attention_values(t):
        e, esink, vals0, vals1 = att.pop(t)
        half = e.shape[0] // 2
        pv0 = jnp.dot(e[:half], vals0, preferred_element_type=F32)
        pv1 = jnp.dot(e[half:], vals1, preferred_element_type=F32)
        for u in range(ATT_GROUP):
            ru = slice(u * CHUNK, (u + 1) * CHUNK)
            a, b = pv0[ru], pv1[ru]
            num = jnp.where(lo, a, b)
            den = pltpu.roll(jnp.where(lo, b, a), HEAD_DIM, 1) + jnp.where(lo, esink[ru], esink[half:][ru])
            col = RET_W + u * LANES
            o_ref[0, rows_of(t), col:col + LANES] = (num * (1.0 / den)).astype(o_ref.dtype)

    def gmlp(t):
        cols = cols_of(t)
        u_act = _gelu_tanh(cols(OFF_CU, CM_W))
        vg = _gelu_tanh(cols(OFF_CV, CM_W))
        vn = jnp.concatenate([_group_norm64(vg[:, j * LANES:(j + 1) * LANES], lo) for j in range(CM_W // LANES)],
                             axis=1) * cng_ref[...]
        sp = jnp.dot(ws_ref[...], per_head_rows(vn), preferred_element_type=F32) + bs_ref[...]
        o_ref[0, rows_of(t), RET_W + ATT_W:] = (u_act * sp).astype(o_ref.dtype)

    for t in range(min(2, r)):
        attention_scores(t)
    for t in range(r):
        attention_softmax(t)
        retention(t)
        if t + 2 < r:
            attention_scores(t + 2)
        attention_values(t)
        gmlp(t)


def _mixer(p, pc, sf, sb, l, tabs, rng, sink, cng, ws, bs, *, local, r):
    b, n, _ = p.shape
    m = pc.shape[1]
    ns = n // (r * CHUNK)
    kv_col = OFF_AK // (2 * KV_W)
    last_chunk = n // CHUNK - 1
    layer = lambda rr, w: pl.BlockSpec((None, rr, w), lambda bi, c: (l, 0, 0))
    in_specs = [pl.BlockSpec((1, r * CHUNK, IN_W), lambda bi, c: (bi, c, 0)),
                pl.BlockSpec((1, m, 2 * KV_W), lambda bi, c: (bi, 0, kv_col)),
                pl.BlockSpec((1, r, HEAD_DIM, RET_W), lambda bi, c: (bi, c, 0, 0)),
                pl.BlockSpec((1, r, HEAD_DIM, RET_W), lambda bi, c: (bi, c, 0, 0)),
                layer(CHUNK, RET_HEADS * CHUNK),
                layer(CHUNK, RET_W), layer(CHUNK, RET_W), layer(1, RET_W),
                pl.BlockSpec(memory_space=pltpu.SMEM),
                layer(1, CM_W),
                layer(CHUNK, CM_GROUPS * CHUNK),
                layer(CHUNK, CM_W)]
    args = [p, pc, sf, sb, tabs["dm"], tabs["qdf"], tabs["qdb"], rng, sink, cng, ws, bs]
    if local:
        in_specs += [pl.BlockSpec((1, CHUNK, 2 * KV_W), lambda bi, c: (bi, jnp.maximum(c * r - 1, 0), kv_col)),
                     pl.BlockSpec((1, CHUNK, 2 * KV_W), lambda bi, c: (bi, jnp.minimum((c + 1) * r, last_chunk), kv_col))]
        args += [p, p]
    return pl.pallas_call(
        functools.partial(_mixer_kernel, local=local, r=r, l=l),
        grid=(b, ns),
        in_specs=in_specs,
        out_specs=pl.BlockSpec((1, r * CHUNK, MIX_W), lambda bi, c: (bi, c, 0)),
        out_shape=jax.ShapeDtypeStruct((b, n, MIX_W), BF16),
        compiler_params=_params(("parallel", "parallel")),
        name="mixer_local" if local else "mixer_ctx",
    )(*args)


def _outffn_kernel(x_ref, mix_ref, mod_ref, g_ref, wo_ref, wg_ref, wu_ref, wd_ref, fg_ref, o_ref, *, th, final):
    a = jnp.dot(mix_ref[0], wo_ref[...], preferred_element_type=F32)
    x1 = x_ref[0] + mod_ref[0, 2:3, :] * a
    y = x1 * lax.rsqrt(jnp.mean(x1 * x1, axis=-1, keepdims=True) + EPS) * g_ref[...]
    z = (y * (1.0 + mod_ref[0, 4:5, :]) + mod_ref[0, 3:4, :]).astype(BF16)
    acc = jnp.zeros(x1.shape, F32)
    for h0 in range(0, wg_ref.shape[1], th):
        hg = jnp.dot(z, wg_ref[:, h0:h0 + th], preferred_element_type=F32)
        hu = jnp.dot(z, wu_ref[:, h0:h0 + th], preferred_element_type=F32)
        acc = acc + jnp.dot((_silu(hg) * hu).astype(BF16), wd_ref[h0:h0 + th, :], preferred_element_type=F32)
    x2 = x1 + mod_ref[0, 5:6, :] * acc
    if final:
        x2 = x2 * lax.rsqrt(jnp.mean(x2 * x2, axis=-1, keepdims=True) + EPS) * fg_ref[...]
    o_ref[0] = x2


def _outffn(x, mix, l, mod, stream_of, g, wo, wg, wu, wd, fg, *, tm, final):
    b, n, d = x.shape
    hid = wg.shape[-1]
    layer = lambda *shape: pl.BlockSpec((None,) + shape, lambda bi, i: (l,) + (0,) * len(shape),
                                        pipeline_mode=pl.Buffered(1))
    return pl.pallas_call(
        functools.partial(_outffn_kernel, th=256, final=final),
        grid=(b, n // tm),
        in_specs=[pl.BlockSpec((1, tm, d), lambda bi, i: (bi, i, 0)),
                  pl.BlockSpec((1, tm, MIX_W), lambda bi, i: (bi, i, 0)),
                  pl.BlockSpec((None, 1, 6, d), lambda bi, i: (l, stream_of(bi), 0, 0)),
                  layer(1, d), layer(MIX_W, d), layer(d, hid), layer(d, hid), layer(hid, d),
                  pl.BlockSpec((1, d), lambda bi, i: (0, 0))],
        out_specs=pl.BlockSpec((1, tm, d), lambda bi, i: (bi, i, 0)),
        out_shape=jax.ShapeDtypeStruct((b, n, d), F32),
        compiler_params=_params(("parallel", "parallel")),
        name="outffn_final" if final else "outffn",
    )(x, mix, mod, g, wo, wg, wu, wd, fg.reshape(1, d))


def _rope_tables(n):
    rows = n // GRID_W
    lane = jnp.arange(LANES)
    inv = 1.0 / (ROPE_BASE ** ((lane % AX_PAIRS).astype(F32) / AX_PAIRS))
    row_lane = ((lane % HEAD_DIM) < 2 * AX_PAIRS)[None, :]
    first = ((lane % (2 * AX_PAIRS)) < AX_PAIRS)[None, :]
    ang_r = jnp.arange(rows, dtype=F32)[:, None] * inv[None, :]
    ang_c = jnp.arange(GRID_W, dtype=F32)[:, None] * inv[None, :]
    both = lambda fr, fc: (jnp.where(row_lane, fr, 0.0)[:, None, :] + jnp.where(row_lane, 0.0, fc)[None, :, :]
                           ).reshape(n, LANES)
    cos = both(jnp.cos(ang_r), jnp.cos(ang_c))
    sin_up = both(jnp.where(first, -jnp.sin(ang_r), 0.0), jnp.where(first, -jnp.sin(ang_c), 0.0))
    sin_dn = both(jnp.where(first, 0.0, jnp.sin(ang_r)), jnp.where(first, 0.0, jnp.sin(ang_c)))
    return cos, sin_up, sin_dn


def _decay_tables(decay_f, decay_b):
    lg_f = jax.nn.log_sigmoid(decay_f.astype(F32))
    lg_b = jax.nn.log_sigmoid(decay_b.astype(F32))
    depth = lg_f.shape[0]
    idx = jnp.arange(CHUNK, dtype=F32)
    diff = idx[:, None] - idx[None, :]
    intra = lambda lg, dd: jnp.where(dd >= 0, jnp.exp(lg[:, :, None, None] * jnp.maximum(dd, 0.0)), 0.0)
    wide = lambda t: jnp.repeat(jnp.swapaxes(t, 1, 2), HEAD_DIM, axis=2)
    return {
        "dm": jnp.swapaxes(intra(lg_f, diff) + intra(lg_b, -diff), 1, 2).reshape(depth, CHUNK, RET_HEADS * CHUNK),
        "qdf": wide(jnp.exp(lg_f[:, :, None] * (idx + 1.0))),
        "qdb": wide(jnp.exp(lg_b[:, :, None] * (CHUNK - idx))),
        "kdf": wide(jnp.exp(lg_f[:, :, None] * (CHUNK - 1.0 - idx))),
        "kdb": wide(jnp.exp(lg_b[:, :, None] * idx)),
        "cdf": wide(jnp.exp(lg_f[:, :, None] * CHUNK)),
        "cdb": wide(jnp.exp(lg_b[:, :, None] * CHUNK)),
    }


def kernel(x, c, ctx, c_ctx, w_mod, b_mod, norm1_g, norm2_g, w_in, ret_decay_f, ret_decay_b, ret_norm_g, attn_sink,
           cm_norm_g, cm_w_s, cm_b_s, w_out, w_gate, w_up, w_down, final_norm_g):
    bsz, n, d = x.shape
    m = ctx.shape[1]
    depth = w_in.shape[0]
    assert n % CHUNK == 0 and m % CHUNK == 0 and d % LANES == 0 and bsz + 1 <= 8
    tm = 512 if n % 512 == 0 else CHUNK

    cin = jnp.zeros((8, d), F32).at[:bsz].set(c).at[bsz].set(c_ctx)
    mod = _modulation(cin, w_mod, b_mod).reshape(depth, 8, 6, d)
    rope = _rope_tables(n)
    no_rope = tuple(jnp.zeros((m, LANES), F32) for _ in range(3))
    zero_state = jnp.zeros((bsz, HEAD_DIM, RET_W), F32)
    latent_stream = lambda bi: bi
    ctx_stream = lambda bi: bsz

    aq = w_in[:, :, OFF_AQ:OFF_AK].reshape(depth, d, ATT_KV_HEADS, ATT_GROUP, HEAD_DIM)
    aq = jnp.swapaxes(aq, 2, 3).reshape(depth, d, ATT_W)
    w_in_b = jnp.concatenate([w_in[:, :, :OFF_AQ], aq, w_in[:, :, OFF_AK:]], axis=2).astype(BF16)
    att = w_out[:, RET_W:RET_W + ATT_W].reshape(depth, ATT_KV_HEADS, ATT_GROUP, HEAD_DIM, d)
    att = jnp.swapaxes(att, 1, 2).reshape(depth, ATT_W, d)
    wo_b = jnp.concatenate([w_out[:, :RET_W], att, w_out[:, RET_W + ATT_W:]], axis=1).astype(BF16)
    wg_b, wu_b, wd_b = w_gate.astype(BF16), w_up.astype(BF16), w_down.astype(BF16)

    tabs = _decay_tables(ret_decay_f, ret_decay_b)
    g1, g2 = norm1_g.reshape(depth, 1, d), norm2_g.reshape(depth, 1, d)
    rng, cng = ret_norm_g.reshape(depth, 1, RET_W), cm_norm_g.reshape(depth, 1, CM_W)
    ws = jnp.swapaxes(cm_w_s, 1, 2).reshape(depth, CHUNK, CM_GROUPS * CHUNK).astype(BF16)
    bs = jnp.repeat(jnp.swapaxes(cm_b_s, 1, 2), HEAD_DIM, axis=2)
    mix_args = (tabs, rng, attn_sink, cng, ws, bs)
    ffn_w = (wo_b, wg_b, wu_b, wd_b, final_norm_g)

    h = ctx
    for l in range(depth):
        last = l == depth - 1
        pc, ufc, ubc = _inproj(h, l, mod, ctx_stream, g1, w_in_b, no_rope, tabs, rope=False, tm=m)
        sfc, sbc, fin_f, fin_b = _ret_scan(ufc, ubc, l, tabs, zero_state, zero_state)
        p, uf, ub = _inproj(x, l, mod, latent_stream, g1, w_in_b, rope, tabs, rope=True, tm=tm)
        sf, sb, _, _ = _ret_scan(uf, ub, l, tabs, fin_f, fin_b)
        mix = _mixer(p, pc, sf, sb, l, *mix_args, local=True, r=4 if n % (4 * CHUNK) == 0 else 1)
        x = _outffn(x, mix, l, mod, latent_stream, g2, *ffn_w, tm=tm, final=last)
        if not last:
            mixc = _mixer(pc, pc, sfc, sbc, l, *mix_args, local=False, r=m // CHUNK)
            h = _outffn(h, mixc, l, mod, ctx_stream, g2, *ffn_w, tm=m, final=False)
    return x
```

```python
import functools

import jax
import jax.numpy as jnp
from jax import lax
from jax.experimental import pallas as pl
from jax.experimental.pallas import tpu as pltpu

F32 = jnp.float32
BF16 = jnp.bfloat16

LANES = 128
HEAD_DIM = 64
RET_HEADS = 4
RET_W = RET_HEADS * HEAD_DIM
ATT_Q_HEADS = 8
ATT_KV_HEADS = 2
ATT_GROUP = ATT_Q_HEADS // ATT_KV_HEADS
ATT_W = ATT_Q_HEADS * HEAD_DIM
KV_W = ATT_KV_HEADS * HEAD_DIM
CM_GROUPS = 4
CM_W = CM_GROUPS * HEAD_DIM
MIX_W = RET_W + ATT_W + CM_W
IN_W = 4 * RET_W + ATT_W + 2 * KV_W + 2 * CM_W
CHUNK = 128
GRID_W = 64
ROPE_BASE = 10000.0
AX_PAIRS = HEAD_DIM // 4
EPS = 1e-6
NEG = -1e30
LOG2E = 1.4426950408889634

OFF_RQ, OFF_RK, OFF_RV, OFF_RG = 0, RET_W, 2 * RET_W, 3 * RET_W
OFF_AQ = 4 * RET_W
OFF_AK = OFF_AQ + ATT_W
OFF_AV = OFF_AK + KV_W
OFF_CU = OFF_AV + KV_W
OFF_CV = OFF_CU + CM_W

HALF_ROWS = 512
VMEM_LIMIT = 56 * 1024 * 1024


def _params(sem):
    return pltpu.CompilerParams(dimension_semantics=sem, vmem_limit_bytes=VMEM_LIMIT)


def _silu(x):
    return x * (1.0 / (1.0 + jnp.exp(-x)))


def _gelu_tanh(x):
    return 0.5 * x * (1.0 + jnp.tanh(0.7978845608028654 * (x + 0.044715 * (x * x * x))))


def _lo_half_mask(shape):
    return (lax.broadcasted_iota(jnp.int32, shape, len(shape) - 1) % LANES) < HEAD_DIM


def _group_norm64(t, lo):
    zero = jnp.zeros_like(t)
    s_lo = jnp.sum(jnp.where(lo, t, zero), axis=-1, keepdims=True)
    s_hi = jnp.sum(jnp.where(lo, zero, t), axis=-1, keepdims=True)
    mu = jnp.where(lo, s_lo, s_hi) * (1.0 / HEAD_DIM)
    d = t - mu
    d2 = d * d
    v_lo = jnp.sum(jnp.where(lo, d2, zero), axis=-1, keepdims=True)
    v_hi = jnp.sum(jnp.where(lo, zero, d2), axis=-1, keepdims=True)
    var = jnp.where(lo, v_lo, v_hi) * (1.0 / HEAD_DIM)
    return d * lax.rsqrt(var + EPS)


def _mod_kernel(c_ref, w_ref, b_ref, o_ref):
    s = _silu(c_ref[...]).astype(BF16)
    o_ref[0] = jnp.dot(s, w_ref[0].astype(BF16), preferred_element_type=F32) + b_ref[0]


def _modulation(cin, w_mod, b_mod):
    depth, d, w6 = w_mod.shape
    tn = 768
    return pl.pallas_call(
        _mod_kernel,
        grid=(depth, w6 // tn),
        in_specs=[pl.BlockSpec((8, d), lambda l, j: (0, 0)),
                  pl.BlockSpec((1, d, tn), lambda l, j: (l, 0, j)),
                  pl.BlockSpec((1, 1, tn), lambda l, j: (l, 0, j))],
        out_specs=pl.BlockSpec((1, 8, tn), lambda l, j: (l, 0, j)),
        out_shape=jax.ShapeDtypeStruct((depth, 8, w6), F32),
        compiler_params=_params(("parallel", "parallel")),
        name="modulation",
    )(cin, w_mod, b_mod.reshape(depth, 1, w6))


def _rope_slab(t, cos, sin_up, sin_dn):
    return t * cos + pltpu.roll(t, LANES - AX_PAIRS, 1) * sin_up + pltpu.roll(t, AX_PAIRS, 1) * sin_dn


def _col_scale(s0):
    if OFF_RK <= s0 < OFF_RV:
        return HEAD_DIM ** -0.5
    if OFF_AQ <= s0 < OFF_AK:
        return HEAD_DIM ** -0.5 * LOG2E
    return None


def _inproj_kernel(x_ref, mod_ref, g_ref, w_ref, cos_ref, sup_ref, sdn_ref, kdf_ref, kdb_ref,
                   o_ref, uf_ref, ub_ref, *, rope):
    tm = x_ref.shape[1]
    nh = max(tm // HALF_ROWS, 1)
    hr = tm // nh
    rope_cols = ((OFF_RQ, OFF_RV), (OFF_AQ, OFF_AV))
    step = 2 * LANES
    zs = {}

    def norm(i):
        x = x_ref[0, i * hr:(i + 1) * hr]
        y = x * lax.rsqrt(jnp.mean(x * x, axis=-1, keepdims=True) + EPS) * g_ref[...]
        zs[i] = (y * (1.0 + mod_ref[0, 1:2, :]) + mod_ref[0, 0:1, :]).astype(BF16)

    def project(i, c0):
        rows = slice(i * hr, (i + 1) * hr)
        acc = jnp.dot(zs[i], w_ref[:, c0:c0 + step], preferred_element_type=F32)
        for s0 in range(c0, c0 + step, LANES):
            t = acc[:, s0 - c0:s0 - c0 + LANES]
            if rope and any(a <= s0 < b for a, b in rope_cols):
                t = _rope_slab(t, cos_ref[rows, :], sup_ref[rows, :], sdn_ref[rows, :])
            if _col_scale(s0) is not None:
                t = t * _col_scale(s0)
            o_ref[0, rows, s0:s0 + LANES] = t

    lo = _lo_half_mask((HEAD_DIM, LANES))

    def increments(i):
        for r in range(i * hr // CHUNK, (i + 1) * hr // CHUNK):
            rows = slice(r * CHUNK, (r + 1) * CHUNK)
            k = o_ref[0, rows, OFF_RK:OFF_RK + RET_W]
            v = o_ref[0, rows, OFF_RV:OFF_RV + RET_W].astype(BF16)
            for kd_ref, u_ref in ((kdf_ref, uf_ref), (kdb_ref, ub_ref)):
                kd = (k * kd_ref[...]).astype(BF16)
                for j in range(RET_W // LANES):
                    sl = slice(j * LANES, (j + 1) * LANES)
                    u = lax.dot_general(kd[:, sl], v[:, sl], (((0,), (0,)), ((), ())), preferred_element_type=F32)
                    u_ref[0, r, :, sl] = jnp.where(lo, u[:HEAD_DIM], u[HEAD_DIM:])

    chunks = list(range(0, IN_W, step))
    norm(0)
    for i in range(nh):
        for n_c, c0 in enumerate(chunks):
            project(i, c0)
            if n_c == 1 and i + 1 < nh:
                norm(i + 1)
        increments(i)


def _inproj(x, l, mod, stream_of, g, w, rope_tabs, tabs, *, rope, tm):
    b, n, d = x.shape
    layer = lambda *shape: pl.BlockSpec((None,) + shape, lambda bi, i: (l,) + (0,) * len(shape))
    cos, sup, sdn = rope_tabs
    nc, cpt = n // CHUNK, tm // CHUNK
    u_spec = pl.BlockSpec((1, cpt, HEAD_DIM, RET_W), lambda bi, i: (bi, i, 0, 0))
    u_shape = jax.ShapeDtypeStruct((b, nc, HEAD_DIM, RET_W), F32)
    return pl.pallas_call(
        functools.partial(_inproj_kernel, rope=rope),
        grid=(b, n // tm),
        in_specs=[pl.BlockSpec((1, tm, d), lambda bi, i: (bi, i, 0)),
                  pl.BlockSpec((None, 1, 6, d), lambda bi, i: (l, stream_of(bi), 0, 0)),
                  layer(1, d), layer(d, IN_W),
                  pl.BlockSpec((tm, LANES), lambda bi, i: (i, 0)),
                  pl.BlockSpec((tm, LANES), lambda bi, i: (i, 0)),
                  pl.BlockSpec((tm, LANES), lambda bi, i: (i, 0)),
                  layer(CHUNK, RET_W), layer(CHUNK, RET_W)],
        out_specs=[pl.BlockSpec((1, tm, IN_W), lambda bi, i: (bi, i, 0)), u_spec, u_spec],
        out_shape=[jax.ShapeDtypeStruct((b, n, IN_W), F32), u_shape, u_shape],
        compiler_params=_params(("parallel", "parallel")),
        name="inproj_rope" if rope else "inproj_ctx",
    )(x, mod, g, w, cos, sup, sdn, tabs["kdf"], tabs["kdb"])


def _scan_kernel(uf_ref, ub_ref, cdf_ref, cdb_ref, s0f_ref, s0b_ref, sf_ref, sb_ref, ff_ref, fb_ref, stf, stb, *, g):
    i = pl.program_id(1)

    @pl.when(i == 0)
    def _():
        stf[...] = s0f_ref[0]
        stb[...] = s0b_ref[0]

    st = stf[...]
    for t in range(g):
        sf_ref[0, t] = st.astype(sf_ref.dtype)
        st = st * cdf_ref[...] + uf_ref[0, t]
    stf[...] = st
    st = stb[...]
    for t in reversed(range(g)):
        sb_ref[0, t] = st.astype(sb_ref.dtype)
        st = st * cdb_ref[...] + ub_ref[0, t]
    stb[...] = st

    @pl.when(i == pl.num_programs(1) - 1)
    def _():
        ff_ref[0] = stf[...]
        fb_ref[0] = stb[...]


def _ret_scan(uf, ub, l, tabs, s0f, s0b):
    b, nc = uf.shape[:2]
    g = max(t for t in range(1, 17) if nc % t == 0)
    ns = nc // g
    blk = (1, g, HEAD_DIM, RET_W)
    st_spec = pl.BlockSpec((1, HEAD_DIM, RET_W), lambda bi, i: (bi, 0, 0))
    tab = pl.BlockSpec((None, 1, RET_W), lambda bi, i: (l, 0, 0))
    return pl.pallas_call(
        functools.partial(_scan_kernel, g=g),
        grid=(b, ns),
        in_specs=[pl.BlockSpec(blk, lambda bi, i: (bi, i, 0, 0)),
                  pl.BlockSpec(blk, lambda bi, i: (bi, ns - 1 - i, 0, 0)),
                  tab, tab, st_spec, st_spec],
        out_specs=[pl.BlockSpec(blk, lambda bi, i: (bi, i, 0, 0)),
                   pl.BlockSpec(blk, lambda bi, i: (bi, ns - 1 - i, 0, 0)),
                   st_spec, st_spec],
        out_shape=[jax.ShapeDtypeStruct((b, nc, HEAD_DIM, RET_W), BF16),
                   jax.ShapeDtypeStruct((b, nc, HEAD_DIM, RET_W), BF16),
                   jax.ShapeDtypeStruct((b, HEAD_DIM, RET_W), F32),
                   jax.ShapeDtypeStruct((b, HEAD_DIM, RET_W), F32)],
        scratch_shapes=[pltpu.VMEM((HEAD_DIM, RET_W), F32), pltpu.VMEM((HEAD_DIM, RET_W), F32)],
        compiler_params=_params(("parallel", "arbitrary")),
        name="ret_scan",
    )(uf, ub, tabs["cdf"], tabs["cdb"], s0f, s0b)


def _mixer_kernel(*refs, local, r, l):
    if local:
        (p_ref, ckv_ref, sf_ref, sb_ref, dm_ref, qdf_ref, qdb_ref, rng_ref, sink_ref, cng_ref, ws_ref, bs_ref,
         kvm_ref, kvp_ref, o_ref) = refs
    else:
        (p_ref, ckv_ref, sf_ref, sb_ref, dm_ref, qdf_ref, qdb_ref, rng_ref, sink_ref, cng_ref, ws_ref, bs_ref,
         o_ref) = refs
    step = pl.program_id(1)
    nsteps = pl.num_programs(1)
    lo = _lo_half_mask((CHUNK, LANES))
    contract_lanes = (((1,), (1,)), ((), ()))
    lane = lax.broadcasted_iota(jnp.int32, (CHUNK, RET_W), 1)
    head_of_lane = [(lane >= h * HEAD_DIM) & (lane < (h + 1) * HEAD_DIM) for h in range(RET_HEADS)]
    lane_s = lax.broadcasted_iota(jnp.int32, (HEAD_DIM, RET_W), 1)
    head_of_state_lane = [(lane_s >= h * HEAD_DIM) & (lane_s < (h + 1) * HEAD_DIM) for h in range(RET_HEADS)]
    if local:
        qi = lax.broadcasted_iota(jnp.int32, (CHUNK, CHUNK), 0)
        kj = lax.broadcasted_iota(jnp.int32, (CHUNK, CHUNK), 1)
    ctx_kv = ckv_ref[0]

    def per_head_rows(t):
        return jnp.concatenate([jnp.where(hm, t, 0.0) for hm in head_of_lane], axis=0).astype(BF16)

    def state_rows(s_ref, t):
        s = s_ref[0, t].astype(F32)
        return jnp.concatenate([jnp.where(hm, s, 0.0) for hm in head_of_state_lane], axis=0).astype(BF16)

    rows_of = lambda t: slice(t * CHUNK, (t + 1) * CHUNK)
    cols_of = lambda t: (lambda off, w: p_ref[0, rows_of(t), off:off + w])
    att = {}

    def retention(t):
        cols = cols_of(t)
        rq, rk, rv = cols(OFF_RQ, RET_W), cols(OFF_RK, RET_W), cols(OFF_RV, RET_W)
        sc = lax.dot_general(rq.astype(BF16), per_head_rows(rk), contract_lanes, preferred_element_type=F32)
        sc = sc * dm_ref[...]
        lhs = jnp.concatenate([sc.astype(BF16), (rq * qdf_ref[...]).astype(BF16), (rq * qdb_ref[...]).astype(BF16)],
                              axis=1)
        rhs = jnp.concatenate([per_head_rows(rv), state_rows(sf_ref, t), state_rows(sb_ref, t)], axis=0)
        o = jnp.dot(lhs, rhs, preferred_element_type=F32)
        for j in range(RET_W // LANES):
            sl = slice(j * LANES, (j + 1) * LANES)
            oj = _group_norm64(o[:, sl], lo) * rng_ref[:, sl]
            o_ref[0, rows_of(t), sl] = (_silu(cols(OFF_RG + j * LANES, LANES)) * oj).astype(o_ref.dtype)

    def attention_scores(t):
        cols = cols_of(t)
        if local:
            kv_of = lambda u: p_ref[0, rows_of(u), OFF_AK:OFF_AK + 2 * KV_W]
            kv_prev = kvm_ref[0] if t == 0 else kv_of(t - 1)
            kv_next = kvp_ref[0] if t == r - 1 else kv_of(t + 1)
            kv = jnp.concatenate([kv_prev, kv_of(t), kv_next, ctx_kv], axis=0)
        else:
            kv = ctx_kv
        nk = kv.shape[0]
        keys = kv[:, :KV_W].astype(BF16)
        lo_k = _lo_half_mask((nk, LANES))
        vals = kv[:, KV_W:]
        vals0 = jnp.where(lo_k, vals, 1.0).astype(BF16)
        vals1 = jnp.where(lo_k, 1.0, vals).astype(BF16)
        qs, sinks = [], []
        for h in range(ATT_Q_HEADS):
            slab = cols(OFF_AQ + (h % ATT_GROUP) * LANES, LANES)
            qs.append(jnp.where(lo, slab, 0.0) if h < ATT_GROUP else jnp.where(lo, 0.0, slab))
            sinks.append(jnp.full((CHUNK, LANES), sink_ref[l, h] * LOG2E, F32))
        q = jnp.concatenate(qs, axis=0).astype(BF16)
        half = q.shape[0] // 2
        s = jnp.concatenate([lax.dot_general(q[:half], keys, contract_lanes, preferred_element_type=F32),
                             lax.dot_general(q[half:], keys, contract_lanes, preferred_element_type=F32)], axis=0)
        att[t] = (s, jnp.concatenate(sinks, axis=0), vals0, vals1)

    def attention_softmax(t):
        s, sink, vals0, vals1 = att[t]
        if local:
            has_prev = jnp.where(step > 0, 0.0, NEG) if t == 0 else 0.0
            has_next = jnp.where(step < nsteps - 1, 0.0, NEG) if t == r - 1 else 0.0
            bias_prev = jnp.concatenate([jnp.where(kj >= qi, has_prev, NEG)] * ATT_Q_HEADS, axis=0)
            bias_next = jnp.concatenate([jnp.where(kj <= qi, has_next, NEG)] * ATT_Q_HEADS, axis=0)
            s = jnp.concatenate([s[:, :CHUNK] + bias_prev, s[:, CHUNK:2 * CHUNK],
                                 s[:, 2 * CHUNK:3 * CHUNK] + bias_next, s[:, 3 * CHUNK:]], axis=1)
        m = jnp.maximum(jnp.broadcast_to(jnp.max(s, axis=-1, keepdims=True), sink.shape), sink)
        e = jnp.exp2(s - jnp.concatenate([m] * (s.shape[1] // LANES), axis=1)).astype(BF16)
        att[t] = (e, jnp.exp2(sink - m), vals0, vals1)

    def attention_values(t):
        e, esink, vals0, vals1 = att.pop(t)
        half = e.shape[0] // 2
        pv0 = jnp.dot(e[:half], vals0, preferred_element_type=F32)
        pv1 = jnp.dot(e[half:], vals1, preferred_element_type=F32)
        for u in range(ATT_GROUP):
            ru = slice(u * CHUNK, (u + 1) * CHUNK)
            a, b = pv0[ru], pv1[ru]
            num = jnp.where(lo, a, b)
            den = pltpu.roll(jnp.where(lo, b, a), HEAD_DIM, 1) + jnp.where(lo, esink[ru], esink[half:][ru])
            col = RET_W + u * LANES
            o_ref[0, rows_of(t), col:col + LANES] = (num * (1.0 / den)).astype(o_ref.dtype)

    def gmlp(t):
        cols = cols_of(t)
        u_act = _gelu_tanh(cols(OFF_CU, CM_W))
        vg = _gelu_tanh(cols(OFF_CV, CM_W))
        vn = jnp.concatenate([_group_norm64(vg[:, j * LANES:(j + 1) * LANES], lo) for j in range(CM_W // LANES)],
                             axis=1) * cng_ref[...]
        sp = jnp.dot(ws_ref[...], per_head_rows(vn), preferred_element_type=F32) + bs_ref[...]
        o_ref[0, rows_of(t), RET_W + ATT_W:] = (u_act * sp).astype(o_ref.dtype)

    for t in range(min(2, r)):
        attention_scores(t)
    for t in range(r):
        attention_softmax(t)
        retention(t)
        if t + 2 < r:
            attention_scores(t + 2)
        attention_values(t)
        gmlp(t)


def _mixer(p, pc, sf, sb, l, tabs, rng, sink, cng, ws, bs, *, local, r):
    b, n, _ = p.shape
    m = pc.shape[1]
    ns = n // (r * CHUNK)
    kv_col = OFF_AK // (2 * KV_W)
    last_chunk = n // CHUNK - 1
    layer = lambda rr, w: pl.BlockSpec((None, rr, w), lambda bi, c: (l, 0, 0))
    in_specs = [pl.BlockSpec((1, r * CHUNK, IN_W), lambda bi, c: (bi, c, 0)),
                pl.BlockSpec((1, m, 2 * KV_W), lambda bi, c: (bi, 0, kv_col)),
                pl.BlockSpec((1, r, HEAD_DIM, RET_W), lambda bi, c: (bi, c, 0, 0)),
                pl.BlockSpec((1, r, HEAD_DIM, RET_W), lambda bi, c: (bi, c, 0, 0)),
                layer(CHUNK, RET_HEADS * CHUNK),
                layer(CHUNK, RET_W), layer(CHUNK, RET_W), layer(1, RET_W),
                pl.BlockSpec(memory_space=pltpu.SMEM),
                layer(1, CM_W),
                layer(CHUNK, CM_GROUPS * CHUNK),
                layer(CHUNK, CM_W)]
    args = [p, pc, sf, sb, tabs["dm"], tabs["qdf"], tabs["qdb"], rng, sink, cng, ws, bs]
    if local:
        in_specs += [pl.BlockSpec((1, CHUNK, 2 * KV_W), lambda bi, c: (bi, jnp.maximum(c * r - 1, 0), kv_col)),
                     pl.BlockSpec((1, CHUNK, 2 * KV_W), lambda bi, c: (bi, jnp.minimum((c + 1) * r, last_chunk), kv_col))]
        args += [p, p]
    return pl.pallas_call(
        functools.partial(_mixer_kernel, local=local, r=r, l=l),
        grid=(b, ns),
        in_specs=in_specs,
        out_specs=pl.BlockSpec((1, r * CHUNK, MIX_W), lambda bi, c: (bi, c, 0)),
        out_shape=jax.ShapeDtypeStruct((b, n, MIX_W), BF16),
        compiler_params=_params(("parallel", "parallel")),
        name="mixer_local" if local else "mixer_ctx",
    )(*args)


def _outffn_kernel(x_ref, mix_ref, mod_ref, g_ref, wo_ref, wg_ref, wu_ref, wd_ref, fg_ref, o_ref, *, th, final):
    tm = x_ref.shape[1]
    nh = max(tm // HALF_ROWS, 1)
    hr = tm // nh
    rows = [slice(i * hr, (i + 1) * hr) for i in range(nh)]
    hid = wg_ref.shape[1]
    nsl = hid // th
    x1s, zs, accs = {}, {}, {}

    def outproj(i):
        a = jnp.dot(mix_ref[0, rows[i]], wo_ref[...], preferred_element_type=F32)
        x1s[i] = x_ref[0, rows[i]] + mod_ref[0, 2:3, :] * a

    def norm(i):
        x1 = x1s[i]
        y = x1 * lax.rsqrt(jnp.mean(x1 * x1, axis=-1, keepdims=True) + EPS) * g_ref[...]
        zs[i] = (y * (1.0 + mod_ref[0, 4:5, :]) + mod_ref[0, 3:4, :]).astype(BF16)
        accs[i] = jnp.zeros(x1.shape, F32)

    def ffn_slice(i, k):
        h0 = k * th
        hg = jnp.dot(zs[i], wg_ref[:, h0:h0 + th], preferred_element_type=F32)
        hu = jnp.dot(zs[i], wu_ref[:, h0:h0 + th], preferred_element_type=F32)
        accs[i] = accs[i] + jnp.dot((_silu(hg) * hu).astype(BF16), wd_ref[h0:h0 + th, :], preferred_element_type=F32)

    def finish(i):
        x2 = x1s[i] + mod_ref[0, 5:6, :] * accs[i]
        if final:
            x2 = x2 * lax.rsqrt(jnp.mean(x2 * x2, axis=-1, keepdims=True) + EPS) * fg_ref[...]
        o_ref[0, rows[i]] = x2

    for i in range(nh):
        outproj(i)
    norm(0)
    for i in range(nh):
        for k in range(nsl):
            ffn_slice(i, k)
            if k == 1 and i + 1 < nh:
                norm(i + 1)
        finish(i)


def _outffn(x, mix, l, mod, stream_of, g, wo, wg, wu, wd, fg, *, tm, final):
    b, n, d = x.shape
    hid = wg.shape[-1]
    layer = lambda *shape: pl.BlockSpec((None,) + shape, lambda bi, i: (l,) + (0,) * len(shape),
                                        pipeline_mode=pl.Buffered(1))
    return pl.pallas_call(
        functools.partial(_outffn_kernel, th=256, final=final),
        grid=(b, n // tm),
        in_specs=[pl.BlockSpec((1, tm, d), lambda bi, i: (bi, i, 0)),
                  pl.BlockSpec((1, tm, MIX_W), lambda bi, i: (bi, i, 0)),
                  pl.BlockSpec((None, 1, 6, d), lambda bi, i: (l, stream_of(bi), 0, 0)),
                  layer(1, d), layer(MIX_W, d), layer(d, hid), layer(d, hid), layer(hid, d),
                  pl.BlockSpec((1, d), lambda bi, i: (0, 0))],
        out_specs=pl.BlockSpec((1, tm, d), lambda bi, i: (bi, i, 0)),
        out_shape=jax.ShapeDtypeStruct((b, n, d), F32),
        compiler_params=_params(("parallel", "parallel")),
        name="outffn_final" if final else "outffn",
    )(x, mix, mod, g, wo, wg, wu, wd, fg.reshape(1, d))


def _rope_tables(n):
    rows = n // GRID_W
    lane = jnp.arange(LANES)
    inv = 1.0 / (ROPE_BASE ** ((lane % AX_PAIRS).astype(F32) / AX_PAIRS))
    row_lane = ((lane % HEAD_DIM) < 2 * AX_PAIRS)[None, :]
    first = ((lane % (2 * AX_PAIRS)) < AX_PAIRS)[None, :]
    ang_r = jnp.arange(rows, dtype=F32)[:, None] * inv[None, :]
    ang_c = jnp.arange(GRID_W, dtype=F32)[:, None] * inv[None, :]
    both = lambda fr, fc: (jnp.where(row_lane, fr, 0.0)[:, None, :] + jnp.where(row_lane, 0.0, fc)[None, :, :]
                           ).reshape(n, LANES)
    cos = both(jnp.cos(ang_r), jnp.cos(ang_c))
    sin_up = both(jnp.where(first, -jnp.sin(ang_r), 0.0), jnp.where(first, -jnp.sin(ang_c), 0.0))
    sin_dn = both(jnp.where(first, 0.0, jnp.sin(ang_r)), jnp.where(first, 0.0, jnp.sin(ang_c)))
    return cos, sin_up, sin_dn


def _decay_tables(decay_f, decay_b):
    lg_f = jax.nn.log_sigmoid(decay_f.astype(F32))
    lg_b = jax.nn.log_sigmoid(decay_b.astype(F32))
    depth = lg_f.shape[0]
    idx = jnp.arange(CHUNK, dtype=F32)
    diff = idx[:, None] - idx[None, :]
    intra = lambda lg, dd: jnp.where(dd >= 0, jnp.exp(lg[:, :, None, None] * jnp.maximum(dd, 0.0)), 0.0)
    wide = lambda t: jnp.repeat(jnp.swapaxes(t, 1, 2), HEAD_DIM, axis=2)
    return {
        "dm": jnp.swapaxes(intra(lg_f, diff) + intra(lg_b, -diff), 1, 2).reshape(depth, CHUNK, RET_HEADS * CHUNK),
        "qdf": wide(jnp.exp(lg_f[:, :, None] * (idx + 1.0))),
        "qdb": wide(jnp.exp(lg_b[:, :, None] * (CHUNK - idx))),
        "kdf": wide(jnp.exp(lg_f[:, :, None] * (CHUNK - 1.0 - idx))),
        "kdb": wide(jnp.exp(lg_b[:, :, None] * idx)),
        "cdf": wide(jnp.exp(lg_f[:, :, None] * CHUNK)),
        "cdb": wide(jnp.exp(lg_b[:, :, None] * CHUNK)),
    }


def kernel(x, c, ctx, c_ctx, w_mod, b_mod, norm1_g, norm2_g, w_in, ret_decay_f, ret_decay_b, ret_norm_g, attn_sink,
           cm_norm_g, cm_w_s, cm_b_s, w_out, w_gate, w_up, w_down, final_norm_g):
    bsz, n, d = x.shape
    m = ctx.shape[1]
    depth = w_in.shape[0]
    assert n % CHUNK == 0 and m % CHUNK == 0 and d % LANES == 0 and bsz + 1 <= 8
    tm = next(t for t in (2 * HALF_ROWS, HALF_ROWS, CHUNK) if n % t == 0)

    cin = jnp.zeros((8, d), F32).at[:bsz].set(c).at[bsz].set(c_ctx)
    mod = _modulation(cin, w_mod, b_mod).reshape(depth, 8, 6, d)
    rope = _rope_tables(n)
    no_rope = tuple(jnp.zeros((m, LANES), F32) for _ in range(3))
    zero_state = jnp.zeros((bsz, HEAD_DIM, RET_W), F32)
    latent_stream = lambda bi: bi
    ctx_stream = lambda bi: bsz

    aq = w_in[:, :, OFF_AQ:OFF_AK].reshape(depth, d, ATT_KV_HEADS, ATT_GROUP, HEAD_DIM)
    aq = jnp.swapaxes(aq, 2, 3).reshape(depth, d, ATT_W)
    w_in_b = jnp.concatenate([w_in[:, :, :OFF_AQ], aq, w_in[:, :, OFF_AK:]], axis=2).astype(BF16)
    att = w_out[:, RET_W:RET_W + ATT_W].reshape(depth, ATT_KV_HEADS, ATT_GROUP, HEAD_DIM, d)
    att = jnp.swapaxes(att, 1, 2).reshape(depth, ATT_W, d)
    wo_b = jnp.concatenate([w_out[:, :RET_W], att, w_out[:, RET_W + ATT_W:]], axis=1).astype(BF16)
    wg_b, wu_b, wd_b = w_gate.astype(BF16), w_up.astype(BF16), w_down.astype(BF16)

    tabs = _decay_tables(ret_decay_f, ret_decay_b)
    g1, g2 = norm1_g.reshape(depth, 1, d), norm2_g.reshape(depth, 1, d)
    rng, cng = ret_norm_g.reshape(depth, 1, RET_W), cm_norm_g.reshape(depth, 1, CM_W)
    ws = jnp.swapaxes(cm_w_s, 1, 2).reshape(depth, CHUNK, CM_GROUPS * CHUNK).astype(BF16)
    bs = jnp.repeat(jnp.swapaxes(cm_b_s, 1, 2), HEAD_DIM, axis=2)
    mix_args = (tabs, rng, attn_sink, cng, ws, bs)
    ffn_w = (wo_b, wg_b, wu_b, wd_b, final_norm_g)

    h = ctx
    for l in range(depth):
        last = l == depth - 1
        pc, ufc, ubc = _inproj(h, l, mod, ctx_stream, g1, w_in_b, no_rope, tabs, rope=False, tm=m)
        sfc, sbc, fin_f, fin_b = _ret_scan(ufc, ubc, l, tabs, zero_state, zero_state)
        p, uf, ub = _inproj(x, l, mod, latent_stream, g1, w_in_b, rope, tabs, rope=True, tm=tm)
        sf, sb, _, _ = _ret_scan(uf, ub, l, tabs, fin_f, fin_b)
        mix = _mixer(p, pc, sf, sb, l, *mix_args, local=True, r=4 if n % (4 * CHUNK) == 0 else 1)
        x = _outffn(x, mix, l, mod, latent_stream, g2, *ffn_w, tm=tm, final=last)
        if not last:
            mixc = _mixer(pc, pc, sfc, sbc, l, *mix_args, local=False, r=m // CHUNK)
            h = _outffn(h, mixc, l, mod, ctx_stream, g2, *ffn_w, tm=m, final=False)
    return x
```

```python
import functools

import jax
import jax.numpy as jnp
from jax import lax
from jax.experimental import pallas as pl
from jax.experimental.pallas import tpu as pltpu

F32 = jnp.float32
BF16 = jnp.bfloat16

LANES = 128
HEAD_DIM = 64
RET_HEADS = 4
RET_W = RET_HEADS * HEAD_DIM
ATT_Q_HEADS = 8
ATT_KV_HEADS = 2
ATT_GROUP = ATT_Q_HEADS // ATT_KV_HEADS
ATT_W = ATT_Q_HEADS * HEAD_DIM
KV_W = ATT_KV_HEADS * HEAD_DIM
CM_GROUPS = 4
CM_W = CM_GROUPS * HEAD_DIM
MIX_W = RET_W + ATT_W + CM_W
IN_W = 4 * RET_W + ATT_W + 2 * KV_W + 2 * CM_W
CHUNK = 128
GRID_W = 64
ROPE_BASE = 10000.0
AX_PAIRS = HEAD_DIM // 4
EPS = 1e-6
NEG = -1e30
LOG2E = 1.4426950408889634

OFF_RQ, OFF_RK, OFF_RV, OFF_RG = 0, RET_W, 2 * RET_W, 3 * RET_W
OFF_AQ = 4 * RET_W
OFF_AK = OFF_AQ + ATT_W
OFF_AV = OFF_AK + KV_W
OFF_CU = OFF_AV + KV_W
OFF_CV = OFF_CU + CM_W

HALF_ROWS = 512
VMEM_LIMIT = 56 * 1024 * 1024


def _params(sem):
    return pltpu.CompilerParams(dimension_semantics=sem, vmem_limit_bytes=VMEM_LIMIT)


def _silu(x):
    return x * (1.0 / (1.0 + jnp.exp(-x)))


def _gelu_tanh(x):
    return 0.5 * x * (1.0 + jnp.tanh(0.7978845608028654 * (x + 0.044715 * (x * x * x))))


def _lo_half_mask(shape):
    return (lax.broadcasted_iota(jnp.int32, shape, len(shape) - 1) % LANES) < HEAD_DIM


def _group_norm64(t, lo):
    zero = jnp.zeros_like(t)
    s_lo = jnp.sum(jnp.where(lo, t, zero), axis=-1, keepdims=True)
    s_hi = jnp.sum(jnp.where(lo, zero, t), axis=-1, keepdims=True)
    mu = jnp.where(lo, s_lo, s_hi) * (1.0 / HEAD_DIM)
    d = t - mu
    d2 = d * d
    v_lo = jnp.sum(jnp.where(lo, d2, zero), axis=-1, keepdims=True)
    v_hi = jnp.sum(jnp.where(lo, zero, d2), axis=-1, keepdims=True)
    var = jnp.where(lo, v_lo, v_hi) * (1.0 / HEAD_DIM)
    return d * lax.rsqrt(var + EPS)


def _mod_kernel(c_ref, w_ref, b_ref, o_ref):
    s = _silu(c_ref[...]).astype(BF16)
    o_ref[0] = jnp.dot(s, w_ref[0].astype(BF16), preferred_element_type=F32) + b_ref[0]


def _modulation(cin, w_mod, b_mod):
    depth, d, w6 = w_mod.shape
    tn = 768
    return pl.pallas_call(
        _mod_kernel,
        grid=(depth, w6 // tn),
        in_specs=[pl.BlockSpec((8, d), lambda l, j: (0, 0)),
                  pl.BlockSpec((1, d, tn), lambda l, j: (l, 0, j)),
                  pl.BlockSpec((1, 1, tn), lambda l, j: (l, 0, j))],
        out_specs=pl.BlockSpec((1, 8, tn), lambda l, j: (l, 0, j)),
        out_shape=jax.ShapeDtypeStruct((depth, 8, w6), F32),
        compiler_params=_params(("parallel", "parallel")),
        name="modulation",
    )(cin, w_mod, b_mod.reshape(depth, 1, w6))


def _rope_slab(t, cos, sin_up, sin_dn):
    return t * cos + pltpu.roll(t, LANES - AX_PAIRS, 1) * sin_up + pltpu.roll(t, AX_PAIRS, 1) * sin_dn


def _col_scale(s0):
    if OFF_RK <= s0 < OFF_RV:
        return HEAD_DIM ** -0.5
    if OFF_AQ <= s0 < OFF_AK:
        return HEAD_DIM ** -0.5 * LOG2E
    return None


def _inproj_kernel(x_ref, mod_ref, g_ref, w_ref, cosr_ref, supr_ref, sdnr_ref, cosc_ref, supc_ref, sdnc_ref,
                   kdf_ref, kdb_ref, o_ref, uf_ref, ub_ref, *, rope):
    tm = x_ref.shape[1]
    nh = max(tm // HALF_ROWS, 1)
    hr = tm // nh
    rope_cols = ((OFF_RQ, OFF_RV), (OFF_AQ, OFF_AV))
    step = 2 * LANES
    zs, rot = {}, {}

    def rope_tiles(i):
        g0 = i * hr // GRID_W
        rot[i] = [jnp.concatenate([r_ref[g0 + g:g0 + g + 1, :] + c_ref[...] for g in range(hr // GRID_W)], axis=0)
                  for r_ref, c_ref in ((cosr_ref, cosc_ref), (supr_ref, supc_ref), (sdnr_ref, sdnc_ref))]

    def norm(i):
        if rope:
            rope_tiles(i)
        x = x_ref[0, i * hr:(i + 1) * hr]
        y = x * lax.rsqrt(jnp.mean(x * x, axis=-1, keepdims=True) + EPS) * g_ref[...]
        zs[i] = (y * (1.0 + mod_ref[0, 1:2, :]) + mod_ref[0, 0:1, :]).astype(BF16)

    def project(i, c0):
        rows = slice(i * hr, (i + 1) * hr)
        acc = jnp.dot(zs[i], w_ref[:, c0:c0 + step], preferred_element_type=F32)
        for s0 in range(c0, c0 + step, LANES):
            t = acc[:, s0 - c0:s0 - c0 + LANES]
            if rope and any(a <= s0 < b for a, b in rope_cols):
                t = _rope_slab(t, *rot[i])
            if _col_scale(s0) is not None:
                t = t * _col_scale(s0)
            o_ref[0, rows, s0:s0 + LANES] = t

    lo = _lo_half_mask((HEAD_DIM, LANES))

    def increments(i):
        for r in range(i * hr // CHUNK, (i + 1) * hr // CHUNK):
            rows = slice(r * CHUNK, (r + 1) * CHUNK)
            k = o_ref[0, rows, OFF_RK:OFF_RK + RET_W]
            v = o_ref[0, rows, OFF_RV:OFF_RV + RET_W].astype(BF16)
            for kd_ref, u_ref in ((kdf_ref, uf_ref), (kdb_ref, ub_ref)):
                kd = (k * kd_ref[...]).astype(BF16)
                for j in range(RET_W // LANES):
                    sl = slice(j * LANES, (j + 1) * LANES)
                    u = lax.dot_general(kd[:, sl], v[:, sl], (((0,), (0,)), ((), ())), preferred_element_type=F32)
                    u_ref[0, r, :, sl] = jnp.where(lo, u[:HEAD_DIM], u[HEAD_DIM:])

    chunks = list(range(0, IN_W, step))
    norm(0)
    for i in range(nh):
        for n_c, c0 in enumerate(chunks):
            project(i, c0)
            if n_c == 1 and i + 1 < nh:
                norm(i + 1)
        increments(i)


def _inproj(x, l, mod, stream_of, g, w, rope_tabs, tabs, *, rope, tm):
    b, n, d = x.shape
    layer = lambda *shape: pl.BlockSpec((None,) + shape, lambda bi, i: (l,) + (0,) * len(shape))
    nc, cpt = n // CHUNK, tm // CHUNK
    row_tab = pl.BlockSpec((tm // GRID_W, LANES), lambda bi, i: (i, 0))
    col_tab = pl.BlockSpec((GRID_W, LANES), lambda bi, i: (0, 0))
    u_spec = pl.BlockSpec((1, cpt, HEAD_DIM, RET_W), lambda bi, i: (bi, i, 0, 0))
    u_shape = jax.ShapeDtypeStruct((b, nc, HEAD_DIM, RET_W), F32)
    return pl.pallas_call(
        functools.partial(_inproj_kernel, rope=rope),
        grid=(b, n // tm),
        in_specs=[pl.BlockSpec((1, tm, d), lambda bi, i: (bi, i, 0)),
                  pl.BlockSpec((None, 1, 6, d), lambda bi, i: (l, stream_of(bi), 0, 0)),
                  layer(1, d), layer(d, IN_W),
                  row_tab, row_tab, row_tab, col_tab, col_tab, col_tab,
                  layer(CHUNK, RET_W), layer(CHUNK, RET_W)],
        out_specs=[pl.BlockSpec((1, tm, IN_W), lambda bi, i: (bi, i, 0)), u_spec, u_spec],
        out_shape=[jax.ShapeDtypeStruct((b, n, IN_W), F32), u_shape, u_shape],
        compiler_params=_params(("parallel", "parallel")),
        name="inproj_rope" if rope else "inproj_ctx",
    )(x, mod, g, w, *rope_tabs, tabs["kdf"], tabs["kdb"])


def _scan_kernel(uf_ref, ub_ref, cdf_ref, cdb_ref, s0f_ref, s0b_ref, sf_ref, sb_ref, ff_ref, fb_ref, stf, stb, *, g):
    i = pl.program_id(1)

    @pl.when(i == 0)
    def _():
        stf[...] = s0f_ref[0]
        stb[...] = s0b_ref[0]

    st = stf[...]
    for t in range(g):
        sf_ref[0, t] = st.astype(sf_ref.dtype)
        st = st * cdf_ref[...] + uf_ref[0, t]
    stf[...] = st
    st = stb[...]
    for t in reversed(range(g)):
        sb_ref[0, t] = st.astype(sb_ref.dtype)
        st = st * cdb_ref[...] + ub_ref[0, t]
    stb[...] = st

    @pl.when(i == pl.num_programs(1) - 1)
    def _():
        ff_ref[0] = stf[...]
        fb_ref[0] = stb[...]


def _ret_scan(uf, ub, l, tabs, s0f, s0b):
    b, nc = uf.shape[:2]
    g = max(t for t in range(1, 17) if nc % t == 0)
    ns = nc // g
    blk = (1, g, HEAD_DIM, RET_W)
    st_spec = pl.BlockSpec((1, HEAD_DIM, RET_W), lambda bi, i: (bi, 0, 0))
    tab = pl.BlockSpec((None, 1, RET_W), lambda bi, i: (l, 0, 0))
    return pl.pallas_call(
        functools.partial(_scan_kernel, g=g),
        grid=(b, ns),
        in_specs=[pl.BlockSpec(blk, lambda bi, i: (bi, i, 0, 0)),
                  pl.BlockSpec(blk, lambda bi, i: (bi, ns - 1 - i, 0, 0)),
                  tab, tab, st_spec, st_spec],
        out_specs=[pl.BlockSpec(blk, lambda bi, i: (bi, i, 0, 0)),
                   pl.BlockSpec(blk, lambda bi, i: (bi, ns - 1 - i, 0, 0)),
                   st_spec, st_spec],
        out_shape=[jax.ShapeDtypeStruct((b, nc, HEAD_DIM, RET_W), BF16),
                   jax.ShapeDtypeStruct((b, nc, HEAD_DIM, RET_W), BF16),
                   jax.ShapeDtypeStruct((b, HEAD_DIM, RET_W), F32),
                   jax.ShapeDtypeStruct((b, HEAD_DIM, RET_W), F32)],
        scratch_shapes=[pltpu.VMEM((HEAD_DIM, RET_W), F32), pltpu.VMEM((HEAD_DIM, RET_W), F32)],
        compiler_params=_params(("parallel", "arbitrary")),
        name="ret_scan",
    )(uf, ub, tabs["cdf"], tabs["cdb"], s0f, s0b)


def _mixer_kernel(*refs, local, r, l):
    if local:
        (p_ref, ckv_ref, sf_ref, sb_ref, dm_ref, qdf_ref, qdb_ref, rng_ref, sink_ref, cng_ref, ws_ref, bs_ref,
         kvm_ref, kvp_ref, o_ref) = refs
    else:
        (p_ref, ckv_ref, sf_ref, sb_ref, dm_ref, qdf_ref, qdb_ref, rng_ref, sink_ref, cng_ref, ws_ref, bs_ref,
         o_ref) = refs
    step = pl.program_id(1)
    nsteps = pl.num_programs(1)
    lo = _lo_half_mask((CHUNK, LANES))
    contract_lanes = (((1,), (1,)), ((), ()))
    lane = lax.broadcasted_iota(jnp.int32, (CHUNK, RET_W), 1)
    head_of_lane = [(lane >= h * HEAD_DIM) & (lane < (h + 1) * HEAD_DIM) for h in range(RET_HEADS)]
    lane_s = lax.broadcasted_iota(jnp.int32, (HEAD_DIM, RET_W), 1)
    head_of_state_lane = [(lane_s >= h * HEAD_DIM) & (lane_s < (h + 1) * HEAD_DIM) for h in range(RET_HEADS)]
    if local:
        qi = lax.broadcasted_iota(jnp.int32, (CHUNK, CHUNK), 0)
        kj = lax.broadcasted_iota(jnp.int32, (CHUNK, CHUNK), 1)
    ctx_kv = ckv_ref[0]

    def per_head_rows(t):
        return jnp.concatenate([jnp.where(hm, t, 0.0) for hm in head_of_lane], axis=0).astype(BF16)

    def state_rows(s_ref, t):
        s = s_ref[0, t].astype(F32)
        return jnp.concatenate([jnp.where(hm, s, 0.0) for hm in head_of_state_lane], axis=0).astype(BF16)

    rows_of = lambda t: slice(t * CHUNK, (t + 1) * CHUNK)
    cols_of = lambda t: (lambda off, w: p_ref[0, rows_of(t), off:off + w])
    att = {}

    def retention(t):
        cols = cols_of(t)
        rq, rk, rv = cols(OFF_RQ, RET_W), cols(OFF_RK, RET_W), cols(OFF_RV, RET_W)
        sc = lax.dot_general(rq.astype(BF16), per_head_rows(rk), contract_lanes, preferred_element_type=F32)
        sc = sc * dm_ref[...]
        lhs = jnp.concatenate([sc.astype(BF16), (rq * qdf_ref[...]).astype(BF16), (rq * qdb_ref[...]).astype(BF16)],
                              axis=1)
        rhs = jnp.concatenate([per_head_rows(rv), state_rows(sf_ref, t), state_rows(sb_ref, t)], axis=0)
        o = jnp.dot(lhs, rhs, preferred_element_type=F32)
        for j in range(RET_W // LANES):
            sl = slice(j * LANES, (j + 1) * LANES)
            oj = _group_norm64(o[:, sl], lo) * rng_ref[:, sl]
            o_ref[0, rows_of(t), sl] = (_silu(cols(OFF_RG + j * LANES, LANES)) * oj).astype(o_ref.dtype)

    def attention_scores(t):
        cols = cols_of(t)
        if local:
            kv_of = lambda u: p_ref[0, rows_of(u), OFF_AK:OFF_AK + 2 * KV_W]
            kv_prev = kvm_ref[0] if t == 0 else kv_of(t - 1)
            kv_next = kvp_ref[0] if t == r - 1 else kv_of(t + 1)
            kv = jnp.concatenate([kv_prev, kv_of(t), kv_next, ctx_kv], axis=0)
        else:
            kv = ctx_kv
        nk = kv.shape[0]
        keys = kv[:, :KV_W].astype(BF16)
        lo_k = _lo_half_mask((nk, LANES))
        vals = kv[:, KV_W:]
        vals0 = jnp.where(lo_k, vals, 1.0).astype(BF16)
        vals1 = jnp.where(lo_k, 1.0, vals).astype(BF16)
        qs, sinks = [], []
        for h in range(ATT_Q_HEADS):
            slab = cols(OFF_AQ + (h % ATT_GROUP) * LANES, LANES)
            qs.append(jnp.where(lo, slab, 0.0) if h < ATT_GROUP else jnp.where(lo, 0.0, slab))
            sinks.append(jnp.full((CHUNK, LANES), sink_ref[l, h] * LOG2E, F32))
        q = jnp.concatenate(qs, axis=0).astype(BF16)
        half = q.shape[0] // 2
        s = jnp.concatenate([lax.dot_general(q[:half], keys, contract_lanes, preferred_element_type=F32),
                             lax.dot_general(q[half:], keys, contract_lanes, preferred_element_type=F32)], axis=0)
        att[t] = (s, jnp.concatenate(sinks, axis=0), vals0, vals1)

    def attention_softmax(t):
        s, sink, vals0, vals1 = att[t]
        if local:
            has_prev = jnp.where(step > 0, 0.0, NEG) if t == 0 else 0.0
            has_next = jnp.where(step < nsteps - 1, 0.0, NEG) if t == r - 1 else 0.0
            bias_prev = jnp.concatenate([jnp.where(kj >= qi, has_prev, NEG)] * ATT_Q_HEADS, axis=0)
            bias_next = jnp.concatenate([jnp.where(kj <= qi, has_next, NEG)] * ATT_Q_HEADS, axis=0)
            s = jnp.concatenate([s[:, :CHUNK] + bias_prev, s[:, CHUNK:2 * CHUNK],
                                 s[:, 2 * CHUNK:3 * CHUNK] + bias_next, s[:, 3 * CHUNK:]], axis=1)
        m = jnp.maximum(jnp.broadcast_to(jnp.max(s, axis=-1, keepdims=True), sink.shape), sink)
        e = jnp.exp2(s - jnp.concatenate([m] * (s.shape[1] // LANES), axis=1)).astype(BF16)
        att[t] = (e, jnp.exp2(sink - m), vals0, vals1)

    def attention_values(t):
        e, esink, vals0, vals1 = att.pop(t)
        half = e.shape[0] // 2
        pv0 = jnp.dot(e[:half], vals0, preferred_element_type=F32)
        pv1 = jnp.dot(e[half:], vals1, preferred_element_type=F32)
        for u in range(ATT_GROUP):
            ru = slice(u * CHUNK, (u + 1) * CHUNK)
            a, b = pv0[ru], pv1[ru]
            num = jnp.where(lo, a, b)
            den = pltpu.roll(jnp.where(lo, b, a), HEAD_DIM, 1) + jnp.where(lo, esink[ru], esink[half:][ru])
            col = RET_W + u * LANES
            o_ref[0, rows_of(t), col:col + LANES] = (num * (1.0 / den)).astype(o_ref.dtype)

    def gmlp(t):
        cols = cols_of(t)
        u_act = _gelu_tanh(cols(OFF_CU, CM_W))
        vg = _gelu_tanh(cols(OFF_CV, CM_W))
        vn = jnp.concatenate([_group_norm64(vg[:, j * LANES:(j + 1) * LANES], lo) for j in range(CM_W // LANES)],
                             axis=1) * cng_ref[...]
        sp = jnp.dot(ws_ref[...], per_head_rows(vn), preferred_element_type=F32) + bs_ref[...]
        o_ref[0, rows_of(t), RET_W + ATT_W:] = (u_act * sp).astype(o_ref.dtype)

    for t in range(min(2, r)):
        attention_scores(t)
    for t in range(r):
        attention_softmax(t)
        retention(t)
        if t + 2 < r:
            attention_scores(t + 2)
        attention_values(t)
        gmlp(t)


def _mixer(p, pc, sf, sb, l, tabs, rng, sink, cng, ws, bs, *, local, r):
    b, n, _ = p.shape
    m = pc.shape[1]
    ns = n // (r * CHUNK)
    kv_col = OFF_AK // (2 * KV_W)
    last_chunk = n // CHUNK - 1
    layer = lambda rr, w: pl.BlockSpec((None, rr, w), lambda bi, c: (l, 0, 0))
    in_specs = [pl.BlockSpec((1, r * CHUNK, IN_W), lambda bi, c: (bi, c, 0)),
                pl.BlockSpec((1, m, 2 * KV_W), lambda bi, c: (bi, 0, kv_col)),
                pl.BlockSpec((1, r, HEAD_DIM, RET_W), lambda bi, c: (bi, c, 0, 0)),
                pl.BlockSpec((1, r, HEAD_DIM, RET_W), lambda bi, c: (bi, c, 0, 0)),
                layer(CHUNK, RET_HEADS * CHUNK),
                layer(CHUNK, RET_W), layer(CHUNK, RET_W), layer(1, RET_W),
                pl.BlockSpec(memory_space=pltpu.SMEM),
                layer(1, CM_W),
                layer(CHUNK, CM_GROUPS * CHUNK),
                layer(CHUNK, CM_W)]
    args = [p, pc, sf, sb, tabs["dm"], tabs["qdf"], tabs["qdb"], rng, sink, cng, ws, bs]
    if local:
        in_specs += [pl.BlockSpec((1, CHUNK, 2 * KV_W), lambda bi, c: (bi, jnp.maximum(c * r - 1, 0), kv_col)),
                     pl.BlockSpec((1, CHUNK, 2 * KV_W), lambda bi, c: (bi, jnp.minimum((c + 1) * r, last_chunk), kv_col))]
        args += [p, p]
    return pl.pallas_call(
        functools.partial(_mixer_kernel, local=local, r=r, l=l),
        grid=(b, ns),
        in_specs=in_specs,
        out_specs=pl.BlockSpec((1, r * CHUNK, MIX_W), lambda bi, c: (bi, c, 0)),
        out_shape=jax.ShapeDtypeStruct((b, n, MIX_W), BF16),
        compiler_params=_params(("parallel", "parallel")),
        name="mixer_local" if local else "mixer_ctx",
    )(*args)


def _outffn_kernel(x_ref, mix_ref, mod_ref, g_ref, wo_ref, wg_ref, wu_ref, wd_ref, fg_ref, o_ref, *, th, final):
    tm = x_ref.shape[1]
    nh = max(tm // HALF_ROWS, 1)
    hr = tm // nh
    rows = [slice(i * hr, (i + 1) * hr) for i in range(nh)]
    hid = wg_ref.shape[1]
    nsl = hid // th
    x1s, zs, accs = {}, {}, {}

    def outproj(i):
        a = jnp.dot(mix_ref[0, rows[i]], wo_ref[...], preferred_element_type=F32)
        x1s[i] = x_ref[0, rows[i]] + mod_ref[0, 2:3, :] * a

    def norm(i):
        x1 = x1s[i]
        y = x1 * lax.rsqrt(jnp.mean(x1 * x1, axis=-1, keepdims=True) + EPS) * g_ref[...]
        zs[i] = (y * (1.0 + mod_ref[0, 4:5, :]) + mod_ref[0, 3:4, :]).astype(BF16)
        accs[i] = jnp.zeros(x1.shape, F32)

    def ffn_slice(i, k):
        h0 = k * th
        hg = jnp.dot(zs[i], wg_ref[:, h0:h0 + th], preferred_element_type=F32)
        hu = jnp.dot(zs[i], wu_ref[:, h0:h0 + th], preferred_element_type=F32)
        accs[i] = accs[i] + jnp.dot((_silu(hg) * hu).astype(BF16), wd_ref[h0:h0 + th, :], preferred_element_type=F32)

    def finish(i):
        x2 = x1s[i] + mod_ref[0, 5:6, :] * accs[i]
        if final:
            x2 = x2 * lax.rsqrt(jnp.mean(x2 * x2, axis=-1, keepdims=True) + EPS) * fg_ref[...]
        o_ref[0, rows[i]] = x2

    for i in range(nh):
        outproj(i)
    norm(0)
    for i in range(nh):
        for k in range(nsl):
            ffn_slice(i, k)
            if k == 1 and i + 1 < nh:
                norm(i + 1)
        finish(i)


def _outffn(x, mix, l, mod, stream_of, g, wo, wg, wu, wd, fg, *, tm, final):
    b, n, d = x.shape
    hid = wg.shape[-1]
    layer = lambda *shape: pl.BlockSpec((None,) + shape, lambda bi, i: (l,) + (0,) * len(shape),
                                        pipeline_mode=pl.Buffered(1))
    return pl.pallas_call(
        functools.partial(_outffn_kernel, th=256, final=final),
        grid=(b, n // tm),
        in_specs=[pl.BlockSpec((1, tm, d), lambda bi, i: (bi, i, 0)),
                  pl.BlockSpec((1, tm, MIX_W), lambda bi, i: (bi, i, 0)),
                  pl.BlockSpec((None, 1, 6, d), lambda bi, i: (l, stream_of(bi), 0, 0)),
                  layer(1, d), layer(MIX_W, d), layer(d, hid), layer(d, hid), layer(hid, d),
                  pl.BlockSpec((1, d), lambda bi, i: (0, 0))],
        out_specs=pl.BlockSpec((1, tm, d), lambda bi, i: (bi, i, 0)),
        out_shape=jax.ShapeDtypeStruct((b, n, d), F32),
        compiler_params=_params(("parallel", "parallel")),
        name="outffn_final" if final else "outffn",
    )(x, mix, mod, g, wo, wg, wu, wd, fg.reshape(1, d))


def _rope_tables(n):
    rows = n // GRID_W
    lane = jnp.arange(LANES)
    inv = 1.0 / (ROPE_BASE ** ((lane % AX_PAIRS).astype(F32) / AX_PAIRS))
    row_lane = ((lane % HEAD_DIM) < 2 * AX_PAIRS)[None, :]
    first = ((lane % (2 * AX_PAIRS)) < AX_PAIRS)[None, :]
    ang_r = jnp.arange(rows, dtype=F32)[:, None] * inv[None, :]
    ang_c = jnp.arange(GRID_W, dtype=F32)[:, None] * inv[None, :]
    on_r = lambda f: jnp.where(row_lane, f, 0.0)
    on_c = lambda f: jnp.where(row_lane, 0.0, f)
    return (on_r(jnp.cos(ang_r)), on_r(jnp.where(first, -jnp.sin(ang_r), 0.0)), on_r(jnp.where(first, 0.0, jnp.sin(ang_r))),
            on_c(jnp.cos(ang_c)), on_c(jnp.where(first, -jnp.sin(ang_c), 0.0)), on_c(jnp.where(first, 0.0, jnp.sin(ang_c))))


def _decay_tables(decay_f, decay_b):
    lg_f = jax.nn.log_sigmoid(decay_f.astype(F32))
    lg_b = jax.nn.log_sigmoid(decay_b.astype(F32))
    depth = lg_f.shape[0]
    idx = jnp.arange(CHUNK, dtype=F32)
    diff = idx[:, None] - idx[None, :]
    intra = lambda lg, dd: jnp.where(dd >= 0, jnp.exp(lg[:, :, None, None] * jnp.maximum(dd, 0.0)), 0.0)
    wide = lambda t: jnp.repeat(jnp.swapaxes(t, 1, 2), HEAD_DIM, axis=2)
    return {
        "dm": jnp.swapaxes(intra(lg_f, diff) + intra(lg_b, -diff), 1, 2).reshape(depth, CHUNK, RET_HEADS * CHUNK),
        "qdf": wide(jnp.exp(lg_f[:, :, None] * (idx + 1.0))),
        "qdb": wide(jnp.exp(lg_b[:, :, None] * (CHUNK - idx))),
        "kdf": wide(jnp.exp(lg_f[:, :, None] * (CHUNK - 1.0 - idx))),
        "kdb": wide(jnp.exp(lg_b[:, :, None] * idx)),
        "cdf": wide(jnp.exp(lg_f[:, :, None] * CHUNK)),
        "cdb": wide(jnp.exp(lg_b[:, :, None] * CHUNK)),
    }


def kernel(x, c, ctx, c_ctx, w_mod, b_mod, norm1_g, norm2_g, w_in, ret_decay_f, ret_decay_b, ret_norm_g, attn_sink,
           cm_norm_g, cm_w_s, cm_b_s, w_out, w_gate, w_up, w_down, final_norm_g):
    bsz, n, d = x.shape
    m = ctx.shape[1]
    depth = w_in.shape[0]
    assert n % CHUNK == 0 and m % CHUNK == 0 and d % LANES == 0 and bsz + 1 <= 8
    tm = next(t for t in (2 * HALF_ROWS, HALF_ROWS, CHUNK) if n % t == 0)

    cin = jnp.zeros((8, d), F32).at[:bsz].set(c).at[bsz].set(c_ctx)
    mod = _modulation(cin, w_mod, b_mod).reshape(depth, 8, 6, d)
    rope = _rope_tables(n)
    mc = bsz * m
    no_rope = tuple(jnp.zeros((r, LANES), F32) for r in (mc // GRID_W,) * 3 + (GRID_W,) * 3)
    zero_state = jnp.zeros((bsz, HEAD_DIM, RET_W), F32)
    latent_stream = lambda bi: bi
    ctx_stream = lambda bi: bsz

    aq = w_in[:, :, OFF_AQ:OFF_AK].reshape(depth, d, ATT_KV_HEADS, ATT_GROUP, HEAD_DIM)
    aq = jnp.swapaxes(aq, 2, 3).reshape(depth, d, ATT_W)
    w_in_b = jnp.concatenate([w_in[:, :, :OFF_AQ], aq, w_in[:, :, OFF_AK:]], axis=2).astype(BF16)
    att = w_out[:, RET_W:RET_W + ATT_W].reshape(depth, ATT_KV_HEADS, ATT_GROUP, HEAD_DIM, d)
    att = jnp.swapaxes(att, 1, 2).reshape(depth, ATT_W, d)
    wo_b = jnp.concatenate([w_out[:, :RET_W], att, w_out[:, RET_W + ATT_W:]], axis=1).astype(BF16)
    wg_b, wu_b, wd_b = w_gate.astype(BF16), w_up.astype(BF16), w_down.astype(BF16)

    tabs = _decay_tables(ret_decay_f, ret_decay_b)
    g1, g2 = norm1_g.reshape(depth, 1, d), norm2_g.reshape(depth, 1, d)
    rng, cng = ret_norm_g.reshape(depth, 1, RET_W), cm_norm_g.reshape(depth, 1, CM_W)
    ws = jnp.swapaxes(cm_w_s, 1, 2).reshape(depth, CHUNK, CM_GROUPS * CHUNK).astype(BF16)
    bs = jnp.repeat(jnp.swapaxes(cm_b_s, 1, 2), HEAD_DIM, axis=2)
    mix_args = (tabs, rng, attn_sink, cng, ws, bs)
    ffn_w = (wo_b, wg_b, wu_b, wd_b, final_norm_g)

    h = ctx
    for l in range(depth):
        last = l == depth - 1
        pc, ufc, ubc = _inproj(h.reshape(1, mc, d), l, mod, ctx_stream, g1, w_in_b, no_rope, tabs, rope=False, tm=mc)
        pc = pc.reshape(bsz, m, IN_W)
        ufc, ubc = (u.reshape(bsz, m // CHUNK, HEAD_DIM, RET_W) for u in (ufc, ubc))
        sfc, sbc, fin_f, fin_b = _ret_scan(ufc, ubc, l, tabs, zero_state, zero_state)
        p, uf, ub = _inproj(x, l, mod, latent_stream, g1, w_in_b, rope, tabs, rope=True, tm=tm)
        sf, sb, _, _ = _ret_scan(uf, ub, l, tabs, fin_f, fin_b)
        mix = _mixer(p, pc, sf, sb, l, *mix_args, local=True, r=4 if n % (4 * CHUNK) == 0 else 1)
        x = _outffn(x, mix, l, mod, latent_stream, g2, *ffn_w, tm=tm, final=last)
        if not last:
            mixc = _mixer(pc, pc, sfc, sbc, l, *mix_args, local=False, r=m // CHUNK)
            h = _outffn(h.reshape(1, mc, d), mixc.reshape(1, mc, MIX_W), l, mod, ctx_stream, g2, *ffn_w, tm=mc,
                        final=False).reshape(bsz, m, d)
    return x
```

```python
import functools

import jax
import jax.numpy as jnp
from jax import lax
from jax.experimental import pallas as pl
from jax.experimental.pallas import tpu as pltpu

F32 = jnp.float32
BF16 = jnp.bfloat16

LANES = 128
HEAD_DIM = 64
RET_HEADS = 4
RET_W = RET_HEADS * HEAD_DIM
ATT_Q_HEADS = 8
ATT_KV_HEADS = 2
ATT_GROUP = ATT_Q_HEADS // ATT_KV_HEADS
ATT_W = ATT_Q_HEADS * HEAD_DIM
KV_W = ATT_KV_HEADS * HEAD_DIM
CM_GROUPS = 4
CM_W = CM_GROUPS * HEAD_DIM
MIX_W = RET_W + ATT_W + CM_W
IN_W = 4 * RET_W + ATT_W + 2 * KV_W + 2 * CM_W
CHUNK = 128
GRID_W = 64
ROPE_BASE = 10000.0
AX_PAIRS = HEAD_DIM // 4
EPS = 1e-6
NEG = -1e30
LOG2E = 1.4426950408889634

OFF_RQ, OFF_RK, OFF_RV, OFF_RG = 0, RET_W, 2 * RET_W, 3 * RET_W
OFF_AQ = 4 * RET_W
OFF_AK = OFF_AQ + ATT_W
OFF_AV = OFF_AK + KV_W
OFF_CU = OFF_AV + KV_W
OFF_CV = OFF_CU + CM_W

HALF_ROWS = 512
VMEM_LIMIT = 56 * 1024 * 1024


def _params(sem):
    return pltpu.CompilerParams(dimension_semantics=sem, vmem_limit_bytes=VMEM_LIMIT)


def _silu(x):
    return x * (1.0 / (1.0 + jnp.exp(-x)))


def _gelu_tanh(x):
    return 0.5 * x * (1.0 + jnp.tanh(0.7978845608028654 * (x + 0.044715 * (x * x * x))))


def _lo_half_mask(shape):
    return (lax.broadcasted_iota(jnp.int32, shape, len(shape) - 1) % LANES) < HEAD_DIM


def _group_norm64(t, lo):
    zero = jnp.zeros_like(t)
    s_lo = jnp.sum(jnp.where(lo, t, zero), axis=-1, keepdims=True)
    s_hi = jnp.sum(jnp.where(lo, zero, t), axis=-1, keepdims=True)
    mu = jnp.where(lo, s_lo, s_hi) * (1.0 / HEAD_DIM)
    d = t - mu
    d2 = d * d
    v_lo = jnp.sum(jnp.where(lo, d2, zero), axis=-1, keepdims=True)
    v_hi = jnp.sum(jnp.where(lo, zero, d2), axis=-1, keepdims=True)
    var = jnp.where(lo, v_lo, v_hi) * (1.0 / HEAD_DIM)
    return d * lax.rsqrt(var + EPS)


def _mod_kernel(c_ref, w_ref, b_ref, o_ref):
    s = _silu(c_ref[...]).astype(BF16)
    o_ref[0] = jnp.dot(s, w_ref[0].astype(BF16), preferred_element_type=F32) + b_ref[0]


def _modulation(cin, w_mod, b_mod):
    depth, d, w6 = w_mod.shape
    tn = 768
    return pl.pallas_call(
        _mod_kernel,
        grid=(depth, w6 // tn),
        in_specs=[pl.BlockSpec((8, d), lambda l, j: (0, 0)),
                  pl.BlockSpec((1, d, tn), lambda l, j: (l, 0, j)),
                  pl.BlockSpec((1, 1, tn), lambda l, j: (l, 0, j))],
        out_specs=pl.BlockSpec((1, 8, tn), lambda l, j: (l, 0, j)),
        out_shape=jax.ShapeDtypeStruct((depth, 8, w6), F32),
        compiler_params=_params(("parallel", "parallel")),
        name="modulation",
    )(cin, w_mod, b_mod.reshape(depth, 1, w6))


def _rope_slab(t, cos, sin_up, sin_dn):
    return t * cos + pltpu.roll(t, LANES - AX_PAIRS, 1) * sin_up + pltpu.roll(t, AX_PAIRS, 1) * sin_dn


def _col_scale(s0):
    if OFF_RK <= s0 < OFF_RV:
        return HEAD_DIM ** -0.5
    if OFF_AQ <= s0 < OFF_AK:
        return HEAD_DIM ** -0.5 * LOG2E
    return None


def _inproj_kernel(x_ref, mod_ref, g_ref, w_ref, cosr_ref, supr_ref, sdnr_ref, cosc_ref, supc_ref, sdnc_ref,
                   kdf_ref, kdb_ref, cdf_ref, s0f_ref, o_ref, sf_ref, ub_ref, ff_ref, stf, *, rope):
    tm = x_ref.shape[1]
    nh = max(tm // HALF_ROWS, 1)
    hr = tm // nh
    rope_cols = ((OFF_RQ, OFF_RV), (OFF_AQ, OFF_AV))
    step = 2 * LANES
    zs, rot = {}, {}

    def rope_tiles(i):
        g0 = i * hr // GRID_W
        rot[i] = [jnp.concatenate([r_ref[g0 + g:g0 + g + 1, :] + c_ref[...] for g in range(hr // GRID_W)], axis=0)
                  for r_ref, c_ref in ((cosr_ref, cosc_ref), (supr_ref, supc_ref), (sdnr_ref, sdnc_ref))]

    def norm(i):
        if rope:
            rope_tiles(i)
        x = x_ref[0, i * hr:(i + 1) * hr]
        y = x * lax.rsqrt(jnp.mean(x * x, axis=-1, keepdims=True) + EPS) * g_ref[...]
        zs[i] = (y * (1.0 + mod_ref[0, 1:2, :]) + mod_ref[0, 0:1, :]).astype(BF16)

    def project(i, c0):
        rows = slice(i * hr, (i + 1) * hr)
        acc = jnp.dot(zs[i], w_ref[:, c0:c0 + step], preferred_element_type=F32)
        for s0 in range(c0, c0 + step, LANES):
            t = acc[:, s0 - c0:s0 - c0 + LANES]
            if rope and any(a <= s0 < b for a, b in rope_cols):
                t = _rope_slab(t, *rot[i])
            if _col_scale(s0) is not None:
                t = t * _col_scale(s0)
            o_ref[0, rows, s0:s0 + LANES] = t

    lo = _lo_half_mask((HEAD_DIM, LANES))

    @pl.when(pl.program_id(1) == 0)
    def _():
        stf[...] = s0f_ref[0]

    def increment(k, v, kd_ref):
        kd = (k * kd_ref[...]).astype(BF16)
        parts = []
        for j in range(RET_W // LANES):
            sl = slice(j * LANES, (j + 1) * LANES)
            u = lax.dot_general(kd[:, sl], v[:, sl], (((0,), (0,)), ((), ())), preferred_element_type=F32)
            parts.append(jnp.where(lo, u[:HEAD_DIM], u[HEAD_DIM:]))
        return jnp.concatenate(parts, axis=1)

    def increments(i):
        st = stf[...]
        for r in range(i * hr // CHUNK, (i + 1) * hr // CHUNK):
            rows = slice(r * CHUNK, (r + 1) * CHUNK)
            k = o_ref[0, rows, OFF_RK:OFF_RK + RET_W]
            v = o_ref[0, rows, OFF_RV:OFF_RV + RET_W].astype(BF16)
            sf_ref[0, r] = st.astype(sf_ref.dtype)
            st = st * cdf_ref[...] + increment(k, v, kdf_ref)
            ub_ref[0, r] = increment(k, v, kdb_ref)
        stf[...] = st

    chunks = list(range(0, IN_W, step))
    norm(0)
    for i in range(nh):
        for n_c, c0 in enumerate(chunks):
            project(i, c0)
            if n_c == 1 and i + 1 < nh:
                norm(i + 1)
        increments(i)

    @pl.when(pl.program_id(1) == pl.num_programs(1) - 1)
    def _():
        ff_ref[0] = stf[...]


def _inproj(x, l, mod, stream_of, g, w, rope_tabs, tabs, s0f, *, rope, tm):
    b, n, d = x.shape
    layer = lambda *shape: pl.BlockSpec((None,) + shape, lambda bi, i: (l,) + (0,) * len(shape))
    nc, cpt = n // CHUNK, tm // CHUNK
    row_tab = pl.BlockSpec((tm // GRID_W, LANES), lambda bi, i: (i, 0))
    col_tab = pl.BlockSpec((GRID_W, LANES), lambda bi, i: (0, 0))
    u_spec = pl.BlockSpec((1, cpt, HEAD_DIM, RET_W), lambda bi, i: (bi, i, 0, 0))
    st_spec = pl.BlockSpec((1, HEAD_DIM, RET_W), lambda bi, i: (bi, 0, 0))
    return pl.pallas_call(
        functools.partial(_inproj_kernel, rope=rope),
        grid=(b, n // tm),
        in_specs=[pl.BlockSpec((1, tm, d), lambda bi, i: (bi, i, 0)),
                  pl.BlockSpec((None, 1, 6, d), lambda bi, i: (l, stream_of(bi), 0, 0)),
                  layer(1, d), layer(d, IN_W),
                  row_tab, row_tab, row_tab, col_tab, col_tab, col_tab,
                  layer(CHUNK, RET_W), layer(CHUNK, RET_W), layer(1, RET_W), st_spec],
        out_specs=[pl.BlockSpec((1, tm, IN_W), lambda bi, i: (bi, i, 0)), u_spec, u_spec, st_spec],
        out_shape=[jax.ShapeDtypeStruct((b, n, IN_W), F32),
                   jax.ShapeDtypeStruct((b, nc, HEAD_DIM, RET_W), BF16),
                   jax.ShapeDtypeStruct((b, nc, HEAD_DIM, RET_W), F32),
                   jax.ShapeDtypeStruct((b, HEAD_DIM, RET_W), F32)],
        scratch_shapes=[pltpu.VMEM((HEAD_DIM, RET_W), F32)],
        compiler_params=_params(("parallel", "arbitrary")),
        name="inproj_rope" if rope else "inproj_ctx",
    )(x, mod, g, w, *rope_tabs, tabs["kdf"], tabs["kdb"], tabs["cdf"], s0f)


def _mixer_kernel(*refs, local, r, l):
    if local:
        (p_ref, ckv_ref, sf_ref, ub_ref, s0b_ref, dm_ref, qdf_ref, qdb_ref, cdb_ref, rng_ref, sink_ref, cng_ref, ws_ref,
         bs_ref, kvm_ref, kvp_ref, o_ref, fb_ref, stb) = refs
    else:
        (p_ref, ckv_ref, sf_ref, ub_ref, s0b_ref, dm_ref, qdf_ref, qdb_ref, cdb_ref, rng_ref, sink_ref, cng_ref, ws_ref,
         bs_ref, o_ref, fb_ref, stb) = refs
    nsteps = pl.num_programs(1)
    step = nsteps - 1 - pl.program_id(1)
    lo = _lo_half_mask((CHUNK, LANES))
    contract_lanes = (((1,), (1,)), ((), ()))
    lane = lax.broadcasted_iota(jnp.int32, (CHUNK, RET_W), 1)
    head_of_lane = [(lane >= h * HEAD_DIM) & (lane < (h + 1) * HEAD_DIM) for h in range(RET_HEADS)]
    lane_s = lax.broadcasted_iota(jnp.int32, (HEAD_DIM, RET_W), 1)
    head_of_state_lane = [(lane_s >= h * HEAD_DIM) & (lane_s < (h + 1) * HEAD_DIM) for h in range(RET_HEADS)]
    if local:
        qi = lax.broadcasted_iota(jnp.int32, (CHUNK, CHUNK), 0)
        kj = lax.broadcasted_iota(jnp.int32, (CHUNK, CHUNK), 1)
    ctx_kv = ckv_ref[0]

    def per_head_rows(t):
        return jnp.concatenate([jnp.where(hm, t, 0.0) for hm in head_of_lane], axis=0).astype(BF16)

    def state_rows(s):
        s = s.astype(F32)
        return jnp.concatenate([jnp.where(hm, s, 0.0) for hm in head_of_state_lane], axis=0).astype(BF16)

    @pl.when(pl.program_id(1) == 0)
    def _():
        stb[...] = s0b_ref[0]

    sb, st_b = {}, stb[...]
    for t in reversed(range(r)):
        sb[t] = st_b
        st_b = st_b * cdb_ref[...] + ub_ref[0, t]
    stb[...] = st_b

    @pl.when(pl.program_id(1) == nsteps - 1)
    def _():
        fb_ref[0] = st_b

    rows_of = lambda t: slice(t * CHUNK, (t + 1) * CHUNK)
    cols_of = lambda t: (lambda off, w: p_ref[0, rows_of(t), off:off + w])
    att = {}

    def retention(t):
        cols = cols_of(t)
        rq, rk, rv = cols(OFF_RQ, RET_W), cols(OFF_RK, RET_W), cols(OFF_RV, RET_W)
        sc = lax.dot_general(rq.astype(BF16), per_head_rows(rk), contract_lanes, preferred_element_type=F32)
        sc = sc * dm_ref[...]
        lhs = jnp.concatenate([sc.astype(BF16), (rq * qdf_ref[...]).astype(BF16), (rq * qdb_ref[...]).astype(BF16)],
                              axis=1)
        rhs = jnp.concatenate([per_head_rows(rv), state_rows(sf_ref[0, t]), state_rows(sb[t])], axis=0)
        o = jnp.dot(lhs, rhs, preferred_element_type=F32)
        for j in range(RET_W // LANES):
            sl = slice(j * LANES, (j + 1) * LANES)
            oj = _group_norm64(o[:, sl], lo) * rng_ref[:, sl]
            o_ref[0, rows_of(t), sl] = (_silu(cols(OFF_RG + j * LANES, LANES)) * oj).astype(o_ref.dtype)

    def attention_scores(t):
        cols = cols_of(t)
        if local:
            kv_of = lambda u: p_ref[0, rows_of(u), OFF_AK:OFF_AK + 2 * KV_W]
            kv_prev = kvm_ref[0] if t == 0 else kv_of(t - 1)
            kv_next = kvp_ref[0] if t == r - 1 else kv_of(t + 1)
            kv = jnp.concatenate([kv_prev, kv_of(t), kv_next, ctx_kv], axis=0)
        else:
            kv = ctx_kv
        nk = kv.shape[0]
        keys = kv[:, :KV_W].astype(BF16)
        lo_k = _lo_half_mask((nk, LANES))
        vals = kv[:, KV_W:]
        vals0 = jnp.where(lo_k, vals, 1.0).astype(BF16)
        vals1 = jnp.where(lo_k, 1.0, vals).astype(BF16)
        qs, sinks = [], []
        for h in range(ATT_Q_HEADS):
            slab = cols(OFF_AQ + (h % ATT_GROUP) * LANES, LANES)
            qs.append(jnp.where(lo, slab, 0.0) if h < ATT_GROUP else jnp.where(lo, 0.0, slab))
            sinks.append(jnp.full((CHUNK, LANES), sink_ref[l, h] * LOG2E, F32))
        q = jnp.concatenate(qs, axis=0).astype(BF16)
        half = q.shape[0] // 2
        s = jnp.concatenate([lax.dot_general(q[:half], keys, contract_lanes, preferred_element_type=F32),
                             lax.dot_general(q[half:], keys, contract_lanes, preferred_element_type=F32)], axis=0)
        att[t] = (s, jnp.concatenate(sinks, axis=0), vals0, vals1)

    def attention_softmax(t):
        s, sink, vals0, vals1 = att[t]
        if local:
            has_prev = jnp.where(step > 0, 0.0, NEG) if t == 0 else 0.0
            has_next = jnp.where(step < nsteps - 1, 0.0, NEG) if t == r - 1 else 0.0
            bias_prev = jnp.concatenate([jnp.where(kj >= qi, has_prev, NEG)] * ATT_Q_HEADS, axis=0)
            bias_next = jnp.concatenate([jnp.where(kj <= qi, has_next, NEG)] * ATT_Q_HEADS, axis=0)
            s = jnp.concatenate([s[:, :CHUNK] + bias_prev, s[:, CHUNK:2 * CHUNK],
                                 s[:, 2 * CHUNK:3 * CHUNK] + bias_next, s[:, 3 * CHUNK:]], axis=1)
        m = jnp.maximum(jnp.broadcast_to(jnp.max(s, axis=-1, keepdims=True), sink.shape), sink)
        e = jnp.exp2(s - jnp.concatenate([m] * (s.shape[1] // LANES), axis=1)).astype(BF16)
        att[t] = (e, jnp.exp2(sink - m), vals0, vals1)

    def attention_values(t):
        e, esink, vals0, vals1 = att.pop(t)
        half = e.shape[0] // 2
        pv0 = jnp.dot(e[:half], vals0, preferred_element_type=F32)
        pv1 = jnp.dot(e[half:], vals1, preferred_element_type=F32)
        for u in range(ATT_GROUP):
            ru = slice(u * CHUNK, (u + 1) * CHUNK)
            a, b = pv0[ru], pv1[ru]
            num = jnp.where(lo, a, b)
            den = pltpu.roll(jnp.where(lo, b, a), HEAD_DIM, 1) + jnp.where(lo, esink[ru], esink[half:][ru])
            col = RET_W + u * LANES
            o_ref[0, rows_of(t), col:col + LANES] = (num * (1.0 / den)).astype(o_ref.dtype)

    def gmlp(t):
        cols = cols_of(t)
        u_act = _gelu_tanh(cols(OFF_CU, CM_W))
        vg = _gelu_tanh(cols(OFF_CV, CM_W))
        vn = jnp.concatenate([_group_norm64(vg[:, j * LANES:(j + 1) * LANES], lo) for j in range(CM_W // LANES)],
                             axis=1) * cng_ref[...]
        sp = jnp.dot(ws_ref[...], per_head_rows(vn), preferred_element_type=F32) + bs_ref[...]
        o_ref[0, rows_of(t), RET_W + ATT_W:] = (u_act * sp).astype(o_ref.dtype)

    for t in range(min(2, r)):
        attention_scores(t)
    for t in range(r):
        attention_softmax(t)
        retention(t)
        if t + 2 < r:
            attention_scores(t + 2)
        attention_values(t)
        gmlp(t)


def _mixer(p, pc, sf, ub, s0b, l, tabs, rng, sink, cng, ws, bs, *, local, r):
    b, n, _ = p.shape
    m = pc.shape[1]
    ns = n // (r * CHUNK)
    kv_col = OFF_AK // (2 * KV_W)
    last_chunk = n // CHUNK - 1
    layer = lambda rr, w: pl.BlockSpec((None, rr, w), lambda bi, c: (l, 0, 0))
    blk = lambda c: ns - 1 - c
    st_spec = pl.BlockSpec((1, HEAD_DIM, RET_W), lambda bi, c: (bi, 0, 0))
    in_specs = [pl.BlockSpec((1, r * CHUNK, IN_W), lambda bi, c: (bi, blk(c), 0)),
                pl.BlockSpec((1, m, 2 * KV_W), lambda bi, c: (bi, 0, kv_col)),
                pl.BlockSpec((1, r, HEAD_DIM, RET_W), lambda bi, c: (bi, blk(c), 0, 0)),
                pl.BlockSpec((1, r, HEAD_DIM, RET_W), lambda bi, c: (bi, blk(c), 0, 0)),
                st_spec,
                layer(CHUNK, RET_HEADS * CHUNK),
                layer(CHUNK, RET_W), layer(CHUNK, RET_W), layer(1, RET_W), layer(1, RET_W),
                pl.BlockSpec(memory_space=pltpu.SMEM),
                layer(1, CM_W),
                layer(CHUNK, CM_GROUPS * CHUNK),
                layer(CHUNK, CM_W)]
    args = [p, pc, sf, ub, s0b, tabs["dm"], tabs["qdf"], tabs["qdb"], tabs["cdb"], rng, sink, cng, ws, bs]
    if local:
        in_specs += [pl.BlockSpec((1, CHUNK, 2 * KV_W), lambda bi, c: (bi, jnp.maximum(blk(c) * r - 1, 0), kv_col)),
                     pl.BlockSpec((1, CHUNK, 2 * KV_W),
                                  lambda bi, c: (bi, jnp.minimum((blk(c) + 1) * r, last_chunk), kv_col))]
        args += [p, p]
    return pl.pallas_call(
        functools.partial(_mixer_kernel, local=local, r=r, l=l),
        grid=(b, ns),
        in_specs=in_specs,
        out_specs=[pl.BlockSpec((1, r * CHUNK, MIX_W), lambda bi, c: (bi, blk(c), 0)), st_spec],
        out_shape=[jax.ShapeDtypeStruct((b, n, MIX_W), BF16), jax.ShapeDtypeStruct((b, HEAD_DIM, RET_W), F32)],
        scratch_shapes=[pltpu.VMEM((HEAD_DIM, RET_W), F32)],
        compiler_params=_params(("parallel", "arbitrary")),
        name="mixer_local" if local else "mixer_ctx",
    )(*args)


def _outffn_kernel(x_ref, mix_ref, mod_ref, g_ref, wo_ref, wg_ref, wu_ref, wd_ref, fg_ref, o_ref, *, th, final):
    tm = x_ref.shape[1]
    nh = max(tm // HALF_ROWS, 1)
    hr = tm // nh
    rows = [slice(i * hr, (i + 1) * hr) for i in range(nh)]
    hid = wg_ref.shape[1]
    nsl = hid // th
    x1s, zs, accs = {}, {}, {}

    def outproj(i):
        a = jnp.dot(mix_ref[0, rows[i]], wo_ref[...], preferred_element_type=F32)
        x1s[i] = x_ref[0, rows[i]] + mod_ref[0, 2:3, :] * a

    def norm(i):
        x1 = x1s[i]
        y = x1 * lax.rsqrt(jnp.mean(x1 * x1, axis=-1, keepdims=True) + EPS) * g_ref[...]
        zs[i] = (y * (1.0 + mod_ref[0, 4:5, :]) + mod_ref[0, 3:4, :]).astype(BF16)
        accs[i] = jnp.zeros(x1.shape, F32)

    def ffn_slice(i, k):
        h0 = k * th
        hg = jnp.dot(zs[i], wg_ref[:, h0:h0 + th], preferred_element_type=F32)
        hu = jnp.dot(zs[i], wu_ref[:, h0:h0 + th], preferred_element_type=F32)
        accs[i] = accs[i] + jnp.dot((_silu(hg) * hu).astype(BF16), wd_ref[h0:h0 + th, :], preferred_element_type=F32)

    def finish(i):
        x2 = x1s[i] + mod_ref[0, 5:6, :] * accs[i]
        if final:
            x2 = x2 * lax.rsqrt(jnp.mean(x2 * x2, axis=-1, keepdims=True) + EPS) * fg_ref[...]
        o_ref[0, rows[i]] = x2

    for i in range(nh):
        outproj(i)
    norm(0)
    for i in range(nh):
        for k in range(nsl):
            ffn_slice(i, k)
            if k == 1 and i + 1 < nh:
                norm(i + 1)
        finish(i)


def _outffn(x, mix, l, mod, stream_of, g, wo, wg, wu, wd, fg, *, tm, final):
    b, n, d = x.shape
    hid = wg.shape[-1]
    layer = lambda *shape: pl.BlockSpec((None,) + shape, lambda bi, i: (l,) + (0,) * len(shape),
                                        pipeline_mode=pl.Buffered(1))
    return pl.pallas_call(
        functools.partial(_outffn_kernel, th=256, final=final),
        grid=(b, n // tm),
        in_specs=[pl.BlockSpec((1, tm, d), lambda bi, i: (bi, i, 0)),
                  pl.BlockSpec((1, tm, MIX_W), lambda bi, i: (bi, i, 0)),
                  pl.BlockSpec((None, 1, 6, d), lambda bi, i: (l, stream_of(bi), 0, 0)),
                  layer(1, d), layer(MIX_W, d), layer(d, hid), layer(d, hid), layer(hid, d),
                  pl.BlockSpec((1, d), lambda bi, i: (0, 0))],
        out_specs=pl.BlockSpec((1, tm, d), lambda bi, i: (bi, i, 0)),
        out_shape=jax.ShapeDtypeStruct((b, n, d), F32),
        compiler_params=_params(("parallel", "parallel")),
        name="outffn_final" if final else "outffn",
    )(x, mix, mod, g, wo, wg, wu, wd, fg.reshape(1, d))


def _rope_tables(n):
    rows = n // GRID_W
    lane = jnp.arange(LANES)
    inv = 1.0 / (ROPE_BASE ** ((lane % AX_PAIRS).astype(F32) / AX_PAIRS))
    row_lane = ((lane % HEAD_DIM) < 2 * AX_PAIRS)[None, :]
    first = ((lane % (2 * AX_PAIRS)) < AX_PAIRS)[None, :]
    ang_r = jnp.arange(rows, dtype=F32)[:, None] * inv[None, :]
    ang_c = jnp.arange(GRID_W, dtype=F32)[:, None] * inv[None, :]
    on_r = lambda f: jnp.where(row_lane, f, 0.0)
    on_c = lambda f: jnp.where(row_lane, 0.0, f)
    return (on_r(jnp.cos(ang_r)), on_r(jnp.where(first, -jnp.sin(ang_r), 0.0)), on_r(jnp.where(first, 0.0, jnp.sin(ang_r))),
            on_c(jnp.cos(ang_c)), on_c(jnp.where(first, -jnp.sin(ang_c), 0.0)), on_c(jnp.where(first, 0.0, jnp.sin(ang_c))))


def _decay_tables(decay_f, decay_b):
    lg_f = jax.nn.log_sigmoid(decay_f.astype(F32))
    lg_b = jax.nn.log_sigmoid(decay_b.astype(F32))
    depth = lg_f.shape[0]
    idx = jnp.arange(CHUNK, dtype=F32)
    diff = idx[:, None] - idx[None, :]
    intra = lambda lg, dd: jnp.where(dd >= 0, jnp.exp(lg[:, :, None, None] * jnp.maximum(dd, 0.0)), 0.0)
    wide = lambda t: jnp.repeat(jnp.swapaxes(t, 1, 2), HEAD_DIM, axis=2)
    return {
        "dm": jnp.swapaxes(intra(lg_f, diff) + intra(lg_b, -diff), 1, 2).reshape(depth, CHUNK, RET_HEADS * CHUNK),
        "qdf": wide(jnp.exp(lg_f[:, :, None] * (idx + 1.0))),
        "qdb": wide(jnp.exp(lg_b[:, :, None] * (CHUNK - idx))),
        "kdf": wide(jnp.exp(lg_f[:, :, None] * (CHUNK - 1.0 - idx))),
        "kdb": wide(jnp.exp(lg_b[:, :, None] * idx)),
        "cdf": wide(jnp.exp(lg_f[:, :, None] * CHUNK)),
        "cdb": wide(jnp.exp(lg_b[:, :, None] * CHUNK)),
    }


def kernel(x, c, ctx, c_ctx, w_mod, b_mod, norm1_g, norm2_g, w_in, ret_decay_f, ret_decay_b, ret_norm_g, attn_sink,
           cm_norm_g, cm_w_s, cm_b_s, w_out, w_gate, w_up, w_down, final_norm_g):
    bsz, n, d = x.shape
    m = ctx.shape[1]
    depth = w_in.shape[0]
    assert n % CHUNK == 0 and m % CHUNK == 0 and d % LANES == 0 and bsz + 1 <= 8
    tm = next(t for t in (2 * HALF_ROWS, HALF_ROWS, CHUNK) if n % t == 0)

    cin = jnp.zeros((8, d), F32).at[:bsz].set(c).at[bsz].set(c_ctx)
    mod = _modulation(cin, w_mod, b_mod).reshape(depth, 8, 6, d)
    rope = _rope_tables(n)
    mc = bsz * m
    no_rope = tuple(jnp.zeros((r, LANES), F32) for r in (m // GRID_W,) * 3 + (GRID_W,) * 3)
    zero_state = jnp.zeros((bsz, HEAD_DIM, RET_W), F32)
    latent_stream = lambda bi: bi
    ctx_stream = lambda bi: bsz

    aq = w_in[:, :, OFF_AQ:OFF_AK].reshape(depth, d, ATT_KV_HEADS, ATT_GROUP, HEAD_DIM)
    aq = jnp.swapaxes(aq, 2, 3).reshape(depth, d, ATT_W)
    w_in_b = jnp.concatenate([w_in[:, :, :OFF_AQ], aq, w_in[:, :, OFF_AK:]], axis=2).astype(BF16)
    att = w_out[:, RET_W:RET_W + ATT_W].reshape(depth, ATT_KV_HEADS, ATT_GROUP, HEAD_DIM, d)
    att = jnp.swapaxes(att, 1, 2).reshape(depth, ATT_W, d)
    wo_b = jnp.concatenate([w_out[:, :RET_W], att, w_out[:, RET_W + ATT_W:]], axis=1).astype(BF16)
    wg_b, wu_b, wd_b = w_gate.astype(BF16), w_up.astype(BF16), w_down.astype(BF16)

    tabs = _decay_tables(ret_decay_f, ret_decay_b)
    g1, g2 = norm1_g.reshape(depth, 1, d), norm2_g.reshape(depth, 1, d)
    rng, cng = ret_norm_g.reshape(depth, 1, RET_W), cm_norm_g.reshape(depth, 1, CM_W)
    ws = jnp.swapaxes(cm_w_s, 1, 2).reshape(depth, CHUNK, CM_GROUPS * CHUNK).astype(BF16)
    bs = jnp.repeat(jnp.swapaxes(cm_b_s, 1, 2), HEAD_DIM, axis=2)
    mix_args = (tabs, rng, attn_sink, cng, ws, bs)
    ffn_w = (wo_b, wg_b, wu_b, wd_b, final_norm_g)

    h = ctx
    for l in range(depth):
        last = l == depth - 1
        pc, sfc, ubc, fin_f = _inproj(h, l, mod, ctx_stream, g1, w_in_b, no_rope, tabs, zero_state, rope=False, tm=m)
        mixc, fin_b = _mixer(pc, pc, sfc, ubc, zero_state, l, *mix_args, local=False, r=m // CHUNK)
        p, sf, ub, _ = _inproj(x, l, mod, latent_stream, g1, w_in_b, rope, tabs, fin_f, rope=True, tm=tm)
        mix, _ = _mixer(p, pc, sf, ub, fin_b, l, *mix_args, local=True, r=4 if n % (4 * CHUNK) == 0 else 1)
        x = _outffn(x, mix, l, mod, latent_stream, g2, *ffn_w, tm=tm, final=last)
        if not last:
            h = _outffn(h.reshape(1, mc, d), mixc.reshape(1, mc, MIX_W), l, mod, ctx_stream, g2, *ffn_w, tm=mc,
                        final=False).reshape(bsz, m, d)
    return x
```

```python
import functools

import jax
import jax.numpy as jnp
from jax import lax
from jax.experimental import pallas as pl
from jax.experimental.pallas import tpu as pltpu

F32 = jnp.float32
BF16 = jnp.bfloat16

LANES = 128
HEAD_DIM = 64
RET_HEADS = 4
RET_W = RET_HEADS * HEAD_DIM
ATT_Q_HEADS = 8
ATT_KV_HEADS = 2
ATT_GROUP = ATT_Q_HEADS // ATT_KV_HEADS
ATT_W = ATT_Q_HEADS * HEAD_DIM
KV_W = ATT_KV_HEADS * HEAD_DIM
CM_GROUPS = 4
CM_W = CM_GROUPS * HEAD_DIM
MIX_W = RET_W + ATT_W + CM_W
IN_W = 4 * RET_W + ATT_W + 2 * KV_W + 2 * CM_W
CHUNK = 128
GRID_W = 64
ROPE_BASE = 10000.0
AX_PAIRS = HEAD_DIM // 4
EPS = 1e-6
NEG = -1e30
LOG2E = 1.4426950408889634

OFF_RQ, OFF_RK, OFF_RV, OFF_RG = 0, RET_W, 2 * RET_W, 3 * RET_W
OFF_AQ = 4 * RET_W
OFF_AK = OFF_AQ + ATT_W
OFF_AV = OFF_AK + KV_W
OFF_CU = OFF_AV + KV_W
OFF_CV = OFF_CU + CM_W

HALF_ROWS = 512
VMEM_LIMIT = 56 * 1024 * 1024


def _params(sem):
    return pltpu.CompilerParams(dimension_semantics=sem, vmem_limit_bytes=VMEM_LIMIT)


def _silu(x):
    return x * (1.0 / (1.0 + jnp.exp(-x)))


def _gelu_tanh(x):
    return 0.5 * x * (1.0 + jnp.tanh(0.7978845608028654 * (x + 0.044715 * (x * x * x))))


def _lo_half_mask(shape):
    return (lax.broadcasted_iota(jnp.int32, shape, len(shape) - 1) % LANES) < HEAD_DIM


def _group_norm64(t, lo):
    zero = jnp.zeros_like(t)
    s_lo = jnp.sum(jnp.where(lo, t, zero), axis=-1, keepdims=True)
    s_hi = jnp.sum(jnp.where(lo, zero, t), axis=-1, keepdims=True)
    mu = jnp.where(lo, s_lo, s_hi) * (1.0 / HEAD_DIM)
    d = t - mu
    d2 = d * d
    v_lo = jnp.sum(jnp.where(lo, d2, zero), axis=-1, keepdims=True)
    v_hi = jnp.sum(jnp.where(lo, zero, d2), axis=-1, keepdims=True)
    var = jnp.where(lo, v_lo, v_hi) * (1.0 / HEAD_DIM)
    return d * lax.rsqrt(var + EPS)


def _mod_kernel(c_ref, w_ref, b_ref, o_ref):
    s = _silu(c_ref[...]).astype(BF16)
    o_ref[0] = jnp.dot(s, w_ref[0].astype(BF16), preferred_element_type=F32) + b_ref[0]


def _modulation(cin, w_mod, b_mod):
    depth, d, w6 = w_mod.shape
    tn = 1536 if w6 % 1536 == 0 else LANES
    return pl.pallas_call(
        _mod_kernel,
        grid=(depth, w6 // tn),
        in_specs=[pl.BlockSpec((8, d), lambda l, j: (0, 0)),
                  pl.BlockSpec((1, d, tn), lambda l, j: (l, 0, j)),
                  pl.BlockSpec((1, 1, tn), lambda l, j: (l, 0, j))],
        out_specs=pl.BlockSpec((1, 8, tn), lambda l, j: (l, 0, j)),
        out_shape=jax.ShapeDtypeStruct((depth, 8, w6), F32),
        compiler_params=_params(("parallel", "parallel")),
        name="modulation",
    )(cin, w_mod, b_mod.reshape(depth, 1, w6))


def _rope_slab(t, cos, sin_up, sin_dn):
    return t * cos + pltpu.roll(t, LANES - AX_PAIRS, 1) * sin_up + pltpu.roll(t, AX_PAIRS, 1) * sin_dn


def _col_scale(s0):
    if OFF_RK <= s0 < OFF_RV:
        return HEAD_DIM ** -0.5
    if OFF_AQ <= s0 < OFF_AK:
        return HEAD_DIM ** -0.5 * LOG2E
    return None


def _inproj_kernel(x_ref, mod_ref, g_ref, w_ref, cosr_ref, supr_ref, sdnr_ref, cosc_ref, supc_ref, sdnc_ref,
                   kdf_ref, kdb_ref, cdf_ref, s0f_ref, o_ref, sf_ref, ub_ref, ff_ref, stf, *, rope):
    tm = x_ref.shape[1]
    nh = max(tm // HALF_ROWS, 1)
    hr = tm // nh
    rope_cols = ((OFF_RQ, OFF_RV), (OFF_AQ, OFF_AV))
    step = 2 * LANES
    zs, rot = {}, {}

    def rope_tiles(i):
        g0 = i * hr // GRID_W
        rot[i] = [jnp.concatenate([r_ref[g0 + g:g0 + g + 1, :] + c_ref[...] for g in range(hr // GRID_W)], axis=0)
                  for r_ref, c_ref in ((cosr_ref, cosc_ref), (supr_ref, supc_ref), (sdnr_ref, sdnc_ref))]

    def norm(i):
        if rope:
            rope_tiles(i)
        x = x_ref[0, i * hr:(i + 1) * hr]
        y = x * lax.rsqrt(jnp.mean(x * x, axis=-1, keepdims=True) + EPS) * g_ref[...]
        zs[i] = (y * (1.0 + mod_ref[0, 1:2, :]) + mod_ref[0, 0:1, :]).astype(BF16)

    def project(i, c0):
        rows = slice(i * hr, (i + 1) * hr)
        acc = jnp.dot(zs[i], w_ref[:, c0:c0 + step], preferred_element_type=F32)
        for s0 in range(c0, c0 + step, LANES):
            t = acc[:, s0 - c0:s0 - c0 + LANES]
            if rope and any(a <= s0 < b for a, b in rope_cols):
                t = _rope_slab(t, *rot[i])
            if _col_scale(s0) is not None:
                t = t * _col_scale(s0)
            o_ref[0, rows, s0:s0 + LANES] = t

    lo = _lo_half_mask((HEAD_DIM, LANES))

    @pl.when(pl.program_id(1) == 0)
    def _():
        stf[...] = s0f_ref[0]

    def increment(k, v):
        kdf = (k * kdf_ref[...]).astype(BF16)
        kdb = (k * kdb_ref[...]).astype(BF16)
        fwd, bwd = [], []
        for j in range(RET_W // LANES):
            sl = slice(j * LANES, (j + 1) * LANES)
            kd = jnp.concatenate([kdf[:, sl], kdb[:, sl]], axis=1)
            u = lax.dot_general(kd, v[:, sl], (((0,), (0,)), ((), ())), preferred_element_type=F32)
            fwd.append(jnp.where(lo, u[:HEAD_DIM], u[HEAD_DIM:LANES]))
            bwd.append(jnp.where(lo, u[LANES:LANES + HEAD_DIM], u[LANES + HEAD_DIM:]))
        return jnp.concatenate(fwd, axis=1), jnp.concatenate(bwd, axis=1)

    def increments(i):
        st = stf[...]
        for r in range(i * hr // CHUNK, (i + 1) * hr // CHUNK):
            rows = slice(r * CHUNK, (r + 1) * CHUNK)
            k = o_ref[0, rows, OFF_RK:OFF_RK + RET_W]
            v = o_ref[0, rows, OFF_RV:OFF_RV + RET_W].astype(BF16)
            sf_ref[0, r] = st.astype(sf_ref.dtype)
            uf, ub_ref[0, r] = increment(k, v)
            st = st * cdf_ref[...] + uf
        stf[...] = st

    chunks = list(range(0, IN_W, step))
    norm(0)
    for i in range(nh):
        for n_c, c0 in enumerate(chunks):
            project(i, c0)
            if n_c == 1 and i + 1 < nh:
                norm(i + 1)
        increments(i)

    @pl.when(pl.program_id(1) == pl.num_programs(1) - 1)
    def _():
        ff_ref[0] = stf[...]


def _inproj(x, l, mod, stream_of, g, w, rope_tabs, tabs, s0f, *, rope, tm):
    b, n, d = x.shape
    layer = lambda *shape: pl.BlockSpec((None,) + shape, lambda bi, i: (l,) + (0,) * len(shape))
    nc, cpt = n // CHUNK, tm // CHUNK
    row_tab = pl.BlockSpec((tm // GRID_W, LANES), lambda bi, i: (i, 0))
    col_tab = pl.BlockSpec((GRID_W, LANES), lambda bi, i: (0, 0))
    u_spec = pl.BlockSpec((1, cpt, HEAD_DIM, RET_W), lambda bi, i: (bi, i, 0, 0))
    st_spec = pl.BlockSpec((1, HEAD_DIM, RET_W), lambda bi, i: (bi, 0, 0))
    return pl.pallas_call(
        functools.partial(_inproj_kernel, rope=rope),
        grid=(b, n // tm),
        in_specs=[pl.BlockSpec((1, tm, d), lambda bi, i: (bi, i, 0)),
                  pl.BlockSpec((None, 1, 6, d), lambda bi, i: (l, stream_of(bi), 0, 0)),
                  layer(1, d), layer(d, IN_W),
                  row_tab, row_tab, row_tab, col_tab, col_tab, col_tab,
                  layer(CHUNK, RET_W), layer(CHUNK, RET_W), layer(1, RET_W), st_spec],
        out_specs=[pl.BlockSpec((1, tm, IN_W), lambda bi, i: (bi, i, 0)), u_spec, u_spec, st_spec],
        out_shape=[jax.ShapeDtypeStruct((b, n, IN_W), F32),
                   jax.ShapeDtypeStruct((b, nc, HEAD_DIM, RET_W), BF16),
                   jax.ShapeDtypeStruct((b, nc, HEAD_DIM, RET_W), F32),
                   jax.ShapeDtypeStruct((b, HEAD_DIM, RET_W), F32)],
        scratch_shapes=[pltpu.VMEM((HEAD_DIM, RET_W), F32)],
        compiler_params=_params(("parallel", "arbitrary")),
        name="inproj_rope" if rope else "inproj_ctx",
    )(x, mod, g, w, *rope_tabs, tabs["kdf"], tabs["kdb"], tabs["cdf"], s0f)


def _mixer_kernel(*refs, local, r, l):
    if local:
        (p_ref, ckv_ref, sf_ref, ub_ref, s0b_ref, dm_ref, qdf_ref, qdb_ref, cdb_ref, rng_ref, sink_ref, cng_ref, ws_ref,
         bs_ref, kvm_ref, kvp_ref, o_ref, fb_ref, stb) = refs
    else:
        (p_ref, ckv_ref, sf_ref, ub_ref, s0b_ref, dm_ref, qdf_ref, qdb_ref, cdb_ref, rng_ref, sink_ref, cng_ref, ws_ref,
         bs_ref, o_ref, fb_ref, stb) = refs
    nsteps = pl.num_programs(1)
    step = nsteps - 1 - pl.program_id(1)
    lo = _lo_half_mask((CHUNK, LANES))
    contract_lanes = (((1,), (1,)), ((), ()))
    lane = lax.broadcasted_iota(jnp.int32, (CHUNK, RET_W), 1)
    head_of_lane = [(lane >= h * HEAD_DIM) & (lane < (h + 1) * HEAD_DIM) for h in range(RET_HEADS)]
    lane_s = lax.broadcasted_iota(jnp.int32, (HEAD_DIM, RET_W), 1)
    head_of_state_lane = [(lane_s >= h * HEAD_DIM) & (lane_s < (h + 1) * HEAD_DIM) for h in range(RET_HEADS)]
    if local:
        qi = lax.broadcasted_iota(jnp.int32, (CHUNK, CHUNK), 0)
        kj = lax.broadcasted_iota(jnp.int32, (CHUNK, CHUNK), 1)
    ctx_kv = ckv_ref[0]

    def per_head_rows(t):
        return jnp.concatenate([jnp.where(hm, t, 0.0) for hm in head_of_lane], axis=0).astype(BF16)

    def state_rows(s):
        s = s.astype(F32)
        return jnp.concatenate([jnp.where(hm, s, 0.0) for hm in head_of_state_lane], axis=0).astype(BF16)

    @pl.when(pl.program_id(1) == 0)
    def _():
        stb[...] = s0b_ref[0]

    sb, st_b = {}, stb[...]
    for t in reversed(range(r)):
        sb[t] = st_b
        st_b = st_b * cdb_ref[...] + ub_ref[0, t]
    stb[...] = st_b

    @pl.when(pl.program_id(1) == nsteps - 1)
    def _():
        fb_ref[0] = st_b

    rows_of = lambda t: slice(t * CHUNK, (t + 1) * CHUNK)
    cols_of = lambda t: (lambda off, w: p_ref[0, rows_of(t), off:off + w])
    att = {}

    def retention(t):
        cols = cols_of(t)
        rq, rk, rv = cols(OFF_RQ, RET_W), cols(OFF_RK, RET_W), cols(OFF_RV, RET_W)
        sc = lax.dot_general(rq.astype(BF16), per_head_rows(rk), contract_lanes, preferred_element_type=F32)
        sc = sc * dm_ref[...]
        lhs = jnp.concatenate([sc.astype(BF16), (rq * qdf_ref[...]).astype(BF16), (rq * qdb_ref[...]).astype(BF16)],
                              axis=1)
        rhs = jnp.concatenate([per_head_rows(rv), state_rows(sf_ref[0, t]), state_rows(sb[t])], axis=0)
        o = jnp.dot(lhs, rhs, preferred_element_type=F32)
        for j in range(RET_W // LANES):
            sl = slice(j * LANES, (j + 1) * LANES)
            oj = _group_norm64(o[:, sl], lo) * rng_ref[:, sl]
            o_ref[0, rows_of(t), sl] = (_silu(cols(OFF_RG + j * LANES, LANES)) * oj).astype(o_ref.dtype)

    def attention_scores(t):
        cols = cols_of(t)
        if local:
            kv_of = lambda u: p_ref[0, rows_of(u), OFF_AK:OFF_AK + 2 * KV_W]
            kv_prev = kvm_ref[0] if t == 0 else kv_of(t - 1)
            kv_next = kvp_ref[0] if t == r - 1 else kv_of(t + 1)
            kv = jnp.concatenate([kv_prev, kv_of(t), kv_next, ctx_kv], axis=0)
        else:
            kv = ctx_kv
        nk = kv.shape[0]
        keys = kv[:, :KV_W].astype(BF16)
        lo_k = _lo_half_mask((nk, LANES))
        vals = kv[:, KV_W:]
        vals0 = jnp.where(lo_k, vals, 1.0).astype(BF16)
        vals1 = jnp.where(lo_k, 1.0, vals).astype(BF16)
        qs, sinks = [], []
        for h in range(ATT_Q_HEADS):
            slab = cols(OFF_AQ + (h % ATT_GROUP) * LANES, LANES)
            qs.append(jnp.where(lo, slab, 0.0) if h < ATT_GROUP else jnp.where(lo, 0.0, slab))
            sinks.append(jnp.full((CHUNK, LANES), sink_ref[l, h] * LOG2E, F32))
        q = jnp.concatenate(qs, axis=0).astype(BF16)
        half = q.shape[0] // 2
        s = jnp.concatenate([lax.dot_general(q[:half], keys, contract_lanes, preferred_element_type=F32),
                             lax.dot_general(q[half:], keys, contract_lanes, preferred_element_type=F32)], axis=0)
        att[t] = (s, jnp.concatenate(sinks, axis=0), vals0, vals1)

    def attention_softmax(t):
        s, sink, vals0, vals1 = att[t]
        if local:
            has_prev = jnp.where(step > 0, 0.0, NEG) if t == 0 else 0.0
            has_next = jnp.where(step < nsteps - 1, 0.0, NEG) if t == r - 1 else 0.0
            bias_prev = jnp.concatenate([jnp.where(kj >= qi, has_prev, NEG)] * ATT_Q_HEADS, axis=0)
            bias_next = jnp.concatenate([jnp.where(kj <= qi, has_next, NEG)] * ATT_Q_HEADS, axis=0)
            s = jnp.concatenate([s[:, :CHUNK] + bias_prev, s[:, CHUNK:2 * CHUNK],
                                 s[:, 2 * CHUNK:3 * CHUNK] + bias_next, s[:, 3 * CHUNK:]], axis=1)
        m = jnp.maximum(jnp.broadcast_to(jnp.max(s, axis=-1, keepdims=True), sink.shape), sink)
        e = jnp.exp2(s - jnp.concatenate([m] * (s.shape[1] // LANES), axis=1)).astype(BF16)
        att[t] = (e, jnp.exp2(sink - m), vals0, vals1)

    def attention_values(t):
        e, esink, vals0, vals1 = att.pop(t)
        half = e.shape[0] // 2
        pv0 = jnp.dot(e[:half], vals0, preferred_element_type=F32)
        pv1 = jnp.dot(e[half:], vals1, preferred_element_type=F32)
        for u in range(ATT_GROUP):
            ru = slice(u * CHUNK, (u + 1) * CHUNK)
            a, b = pv0[ru], pv1[ru]
            num = jnp.where(lo, a, b)
            den = pltpu.roll(jnp.where(lo, b, a), HEAD_DIM, 1) + jnp.where(lo, esink[ru], esink[half:][ru])
            col = RET_W + u * LANES
            o_ref[0, rows_of(t), col:col + LANES] = (num * (1.0 / den)).astype(o_ref.dtype)

    def gmlp(t):
        cols = cols_of(t)
        u_act = _gelu_tanh(cols(OFF_CU, CM_W))
        vg = _gelu_tanh(cols(OFF_CV, CM_W))
        vn = jnp.concatenate([_group_norm64(vg[:, j * LANES:(j + 1) * LANES], lo) for j in range(CM_W // LANES)],
                             axis=1) * cng_ref[...]
        sp = jnp.dot(ws_ref[...], per_head_rows(vn), preferred_element_type=F32) + bs_ref[...]
        o_ref[0, rows_of(t), RET_W + ATT_W:] = (u_act * sp).astype(o_ref.dtype)

    for t in range(min(2, r)):
        attention_scores(t)
    for t in range(r):
        attention_softmax(t)
        retention(t)
        if t + 2 < r:
            attention_scores(t + 2)
        attention_values(t)
        gmlp(t)


def _mixer(p, pc, sf, ub, s0b, l, tabs, rng, sink, cng, ws, bs, *, local, r):
    b, n, _ = p.shape
    m = pc.shape[1]
    ns = n // (r * CHUNK)
    kv_col = OFF_AK // (2 * KV_W)
    last_chunk = n // CHUNK - 1
    layer = lambda rr, w: pl.BlockSpec((None, rr, w), lambda bi, c: (l, 0, 0))
    blk = lambda c: ns - 1 - c
    st_spec = pl.BlockSpec((1, HEAD_DIM, RET_W), lambda bi, c: (bi, 0, 0))
    in_specs = [pl.BlockSpec((1, r * CHUNK, IN_W), lambda bi, c: (bi, blk(c), 0)),
                pl.BlockSpec((1, m, 2 * KV_W), lambda bi, c: (bi, 0, kv_col)),
                pl.BlockSpec((1, r, HEAD_DIM, RET_W), lambda bi, c: (bi, blk(c), 0, 0)),
                pl.BlockSpec((1, r, HEAD_DIM, RET_W), lambda bi, c: (bi, blk(c), 0, 0)),
                st_spec,
                layer(CHUNK, RET_HEADS * CHUNK),
                layer(CHUNK, RET_W), layer(CHUNK, RET_W), layer(1, RET_W), layer(1, RET_W),
                pl.BlockSpec(memory_space=pltpu.SMEM),
                layer(1, CM_W),
                layer(CHUNK, CM_GROUPS * CHUNK),
                layer(CHUNK, CM_W)]
    args = [p, pc, sf, ub, s0b, tabs["dm"], tabs["qdf"], tabs["qdb"], tabs["cdb"], rng, sink, cng, ws, bs]
    if local:
        in_specs += [pl.BlockSpec((1, CHUNK, 2 * KV_W), lambda bi, c: (bi, jnp.maximum(blk(c) * r - 1, 0), kv_col)),
                     pl.BlockSpec((1, CHUNK, 2 * KV_W),
                                  lambda bi, c: (bi, jnp.minimum((blk(c) + 1) * r, last_chunk), kv_col))]
        args += [p, p]
    return pl.pallas_call(
        functools.partial(_mixer_kernel, local=local, r=r, l=l),
        grid=(b, ns),
        in_specs=in_specs,
        out_specs=[pl.BlockSpec((1, r * CHUNK, MIX_W), lambda bi, c: (bi, blk(c), 0)), st_spec],
        out_shape=[jax.ShapeDtypeStruct((b, n, MIX_W), BF16), jax.ShapeDtypeStruct((b, HEAD_DIM, RET_W), F32)],
        scratch_shapes=[pltpu.VMEM((HEAD_DIM, RET_W), F32)],
        compiler_params=_params(("parallel", "arbitrary")),
        name="mixer_local" if local else "mixer_ctx",
    )(*args)


def _outffn_kernel(x_ref, mix_ref, mod_ref, g_ref, wo_ref, wg_ref, wu_ref, wd_ref, fg_ref, o_ref, *, th, final):
    tm = x_ref.shape[1]
    nh = max(tm // HALF_ROWS, 1)
    hr = tm // nh
    rows = [slice(i * hr, (i + 1) * hr) for i in range(nh)]
    hid = wg_ref.shape[1]
    nsl = hid // th
    x1s, zs, accs = {}, {}, {}

    def outproj(i):
        a = jnp.dot(mix_ref[0, rows[i]], wo_ref[...], preferred_element_type=F32)
        x1s[i] = x_ref[0, rows[i]] + mod_ref[0, 2:3, :] * a

    def norm(i):
        x1 = x1s[i]
        y = x1 * lax.rsqrt(jnp.mean(x1 * x1, axis=-1, keepdims=True) + EPS) * g_ref[...]
        zs[i] = (y * (1.0 + mod_ref[0, 4:5, :]) + mod_ref[0, 3:4, :]).astype(BF16)
        accs[i] = jnp.zeros(x1.shape, F32)

    def ffn_slice(i, k):
        h0 = k * th
        hg = jnp.dot(zs[i], wg_ref[:, h0:h0 + th], preferred_element_type=F32)
        hu = jnp.dot(zs[i], wu_ref[:, h0:h0 + th], preferred_element_type=F32)
        accs[i] = accs[i] + jnp.dot((_silu(hg) * hu).astype(BF16), wd_ref[h0:h0 + th, :], preferred_element_type=F32)

    def finish(i):
        x2 = x1s[i] + mod_ref[0, 5:6, :] * accs[i]
        if final:
            x2 = x2 * lax.rsqrt(jnp.mean(x2 * x2, axis=-1, keepdims=True) + EPS) * fg_ref[...]
        o_ref[0, rows[i]] = x2

    for i in range(nh):
        outproj(i)
    norm(0)
    for i in range(nh):
        for k in range(nsl):
            ffn_slice(i, k)
            if k == 1 and i + 1 < nh:
                norm(i + 1)
        finish(i)


def _outffn(x, mix, l, mod, stream_of, g, wo, wg, wu, wd, fg, *, tm, final):
    b, n, d = x.shape
    hid = wg.shape[-1]
    layer = lambda *shape: pl.BlockSpec((None,) + shape, lambda bi, i: (l,) + (0,) * len(shape),
                                        pipeline_mode=pl.Buffered(1))
    return pl.pallas_call(
        functools.partial(_outffn_kernel, th=256, final=final),
        grid=(b, n // tm),
        in_specs=[pl.BlockSpec((1, tm, d), lambda bi, i: (bi, i, 0)),
                  pl.BlockSpec((1, tm, MIX_W), lambda bi, i: (bi, i, 0)),
                  pl.BlockSpec((None, 1, 6, d), lambda bi, i: (l, stream_of(bi), 0, 0)),
                  layer(1, d), layer(MIX_W, d), layer(d, hid), layer(d, hid), layer(hid, d),
                  pl.BlockSpec((1, d), lambda bi, i: (0, 0))],
        out_specs=pl.BlockSpec((1, tm, d), lambda bi, i: (bi, i, 0)),
        out_shape=jax.ShapeDtypeStruct((b, n, d), F32),
        compiler_params=_params(("parallel", "parallel")),
        name="outffn_final" if final else "outffn",
    )(x, mix, mod, g, wo, wg, wu, wd, fg.reshape(1, d))


def _rope_tables(n):
    rows = n // GRID_W
    lane = jnp.arange(LANES)
    inv = 1.0 / (ROPE_BASE ** ((lane % AX_PAIRS).astype(F32) / AX_PAIRS))
    row_lane = ((lane % HEAD_DIM) < 2 * AX_PAIRS)[None, :]
    first = ((lane % (2 * AX_PAIRS)) < AX_PAIRS)[None, :]
    ang_r = jnp.arange(rows, dtype=F32)[:, None] * inv[None, :]
    ang_c = jnp.arange(GRID_W, dtype=F32)[:, None] * inv[None, :]
    on_r = lambda f: jnp.where(row_lane, f, 0.0)
    on_c = lambda f: jnp.where(row_lane, 0.0, f)
    return (on_r(jnp.cos(ang_r)), on_r(jnp.where(first, -jnp.sin(ang_r), 0.0)), on_r(jnp.where(first, 0.0, jnp.sin(ang_r))),
            on_c(jnp.cos(ang_c)), on_c(jnp.where(first, -jnp.sin(ang_c), 0.0)), on_c(jnp.where(first, 0.0, jnp.sin(ang_c))))


def _decay_tables(decay_f, decay_b):
    lg_f = jax.nn.log_sigmoid(decay_f.astype(F32))
    lg_b = jax.nn.log_sigmoid(decay_b.astype(F32))
    depth = lg_f.shape[0]
    idx = jnp.arange(CHUNK, dtype=F32)
    diff = idx[:, None] - idx[None, :]
    intra = lambda lg, dd: jnp.where(dd >= 0, jnp.exp(lg[:, :, None, None] * jnp.maximum(dd, 0.0)), 0.0)
    wide = lambda t: jnp.repeat(jnp.swapaxes(t, 1, 2), HEAD_DIM, axis=2)
    return {
        "dm": jnp.swapaxes(intra(lg_f, diff) + intra(lg_b, -diff), 1, 2).reshape(depth, CHUNK, RET_HEADS * CHUNK),
        "qdf": wide(jnp.exp(lg_f[:, :, None] * (idx + 1.0))),
        "qdb": wide(jnp.exp(lg_b[:, :, None] * (CHUNK - idx))),
        "kdf": wide(jnp.exp(lg_f[:, :, None] * (CHUNK - 1.0 - idx))),
        "kdb": wide(jnp.exp(lg_b[:, :, None] * idx)),
        "cdf": wide(jnp.exp(lg_f[:, :, None] * CHUNK)),
        "cdb": wide(jnp.exp(lg_b[:, :, None] * CHUNK)),
    }


def kernel(x, c, ctx, c_ctx, w_mod, b_mod, norm1_g, norm2_g, w_in, ret_decay_f, ret_decay_b, ret_norm_g, attn_sink,
           cm_norm_g, cm_w_s, cm_b_s, w_out, w_gate, w_up, w_down, final_norm_g):
    bsz, n, d = x.shape
    m = ctx.shape[1]
    depth = w_in.shape[0]
    assert n % CHUNK == 0 and m % CHUNK == 0 and d % LANES == 0 and bsz + 1 <= 8
    tm = next(t for t in (2 * HALF_ROWS, HALF_ROWS, CHUNK) if n % t == 0)

    cin = jnp.zeros((8, d), F32).at[:bsz].set(c).at[bsz].set(c_ctx)
    mod = _modulation(cin, w_mod, b_mod).reshape(depth, 8, 6, d)
    rope = _rope_tables(n)
    mc = bsz * m
    no_rope = tuple(jnp.zeros((r, LANES), F32) for r in (m // GRID_W,) * 3 + (GRID_W,) * 3)
    zero_state = jnp.zeros((bsz, HEAD_DIM, RET_W), F32)
    latent_stream = lambda bi: bi
    ctx_stream = lambda bi: bsz

    w_in_c, w_out_c = w_in.astype(BF16), w_out.astype(BF16)
    aq = w_in_c[:, :, OFF_AQ:OFF_AK].reshape(depth, d, ATT_KV_HEADS, ATT_GROUP, HEAD_DIM)
    aq = jnp.swapaxes(aq, 2, 3).reshape(depth, d, ATT_W)
    w_in_b = jnp.concatenate([w_in_c[:, :, :OFF_AQ], aq, w_in_c[:, :, OFF_AK:]], axis=2)
    att = w_out_c[:, RET_W:RET_W + ATT_W].reshape(depth, ATT_KV_HEADS, ATT_GROUP, HEAD_DIM, d)
    att = jnp.swapaxes(att, 1, 2).reshape(depth, ATT_W, d)
    wo_b = jnp.concatenate([w_out_c[:, :RET_W], att, w_out_c[:, RET_W + ATT_W:]], axis=1)
    wg_b, wu_b, wd_b = w_gate.astype(BF16), w_up.astype(BF16), w_down.astype(BF16)

    tabs = _decay_tables(ret_decay_f, ret_decay_b)
    g1, g2 = norm1_g.reshape(depth, 1, d), norm2_g.reshape(depth, 1, d)
    rng, cng = ret_norm_g.reshape(depth, 1, RET_W), cm_norm_g.reshape(depth, 1, CM_W)
    ws = jnp.swapaxes(cm_w_s, 1, 2).reshape(depth, CHUNK, CM_GROUPS * CHUNK).astype(BF16)
    bs = jnp.repeat(jnp.swapaxes(cm_b_s, 1, 2), HEAD_DIM, axis=2)
    mix_args = (tabs, rng, attn_sink, cng, ws, bs)
    ffn_w = (wo_b, wg_b, wu_b, wd_b, final_norm_g)

    h = ctx
    for l in range(depth):
        last = l == depth - 1
        pc, sfc, ubc, fin_f = _inproj(h, l, mod, ctx_stream, g1, w_in_b, no_rope, tabs, zero_state, rope=False, tm=m)
        mixc, fin_b = _mixer(pc, pc, sfc, ubc, zero_state, l, *mix_args, local=False, r=m // CHUNK)
        p, sf, ub, _ = _inproj(x, l, mod, latent_stream, g1, w_in_b, rope, tabs, fin_f, rope=True, tm=tm)
        mix, _ = _mixer(p, pc, sf, ub, fin_b, l, *mix_args, local=True, r=4 if n % (4 * CHUNK) == 0 else 1)
        x = _outffn(x, mix, l, mod, latent_stream, g2, *ffn_w, tm=tm, final=last)
        if not last:
            h = _outffn(h.reshape(1, mc, d), mixc.reshape(1, mc, MIX_W), l, mod, ctx_stream, g2, *ffn_w, tm=mc,
                        final=False).reshape(bsz, m, d)
    return x
```

```python
import functools

import jax
import jax.numpy as jnp
from jax import lax
from jax.experimental import pallas as pl
from jax.experimental.pallas import tpu as pltpu

F32 = jnp.float32
BF16 = jnp.bfloat16

LANES = 128
HEAD_DIM = 64
RET_HEADS = 4
RET_W = RET_HEADS * HEAD_DIM
ATT_Q_HEADS = 8
ATT_KV_HEADS = 2
ATT_GROUP = ATT_Q_HEADS // ATT_KV_HEADS
ATT_W = ATT_Q_HEADS * HEAD_DIM
KV_W = ATT_KV_HEADS * HEAD_DIM
CM_GROUPS = 4
CM_W = CM_GROUPS * HEAD_DIM
MIX_W = RET_W + ATT_W + CM_W
IN_W = 4 * RET_W + ATT_W + 2 * KV_W + 2 * CM_W
CHUNK = 128
GRID_W = 64
ROPE_BASE = 10000.0
AX_PAIRS = HEAD_DIM // 4
EPS = 1e-6
NEG = -1e30
LOG2E = 1.4426950408889634

OFF_RQ, OFF_RK, OFF_RV, OFF_RG = 0, RET_W, 2 * RET_W, 3 * RET_W
OFF_AQ = 4 * RET_W
OFF_AK = OFF_AQ + ATT_W
OFF_AV = OFF_AK + KV_W
OFF_CU = OFF_AV + KV_W
OFF_CV = OFF_CU + CM_W

HALF_ROWS = 512
VMEM_LIMIT = 56 * 1024 * 1024


def _params(sem):
    return pltpu.CompilerParams(dimension_semantics=sem, vmem_limit_bytes=VMEM_LIMIT)


def _silu(x):
    return x * (1.0 / (1.0 + jnp.exp(-x)))


def _gelu_tanh(x):
    return 0.5 * x * (1.0 + jnp.tanh(0.7978845608028654 * (x + 0.044715 * (x * x * x))))


def _lo_half_mask(shape):
    return (lax.broadcasted_iota(jnp.int32, shape, len(shape) - 1) % LANES) < HEAD_DIM


def _group_norm64(t, lo):
    zero = jnp.zeros_like(t)
    s_lo = jnp.sum(jnp.where(lo, t, zero), axis=-1, keepdims=True)
    s_hi = jnp.sum(jnp.where(lo, zero, t), axis=-1, keepdims=True)
    mu = jnp.where(lo, s_lo, s_hi) * (1.0 / HEAD_DIM)
    d = t - mu
    d2 = d * d
    v_lo = jnp.sum(jnp.where(lo, d2, zero), axis=-1, keepdims=True)
    v_hi = jnp.sum(jnp.where(lo, zero, d2), axis=-1, keepdims=True)
    var = jnp.where(lo, v_lo, v_hi) * (1.0 / HEAD_DIM)
    return d * lax.rsqrt(var + EPS)


def _mod_kernel(c_ref, w_ref, b_ref, o_ref):
    s = _silu(c_ref[...]).astype(BF16)
    o_ref[0] = jnp.dot(s, w_ref[0].astype(BF16), preferred_element_type=F32) + b_ref[0]


def _modulation(cin, w_mod, b_mod):
    depth, d, w6 = w_mod.shape
    tn = 1536 if w6 % 1536 == 0 else LANES
    return pl.pallas_call(
        _mod_kernel,
        grid=(depth, w6 // tn),
        in_specs=[pl.BlockSpec((8, d), lambda l, j: (0, 0)),
                  pl.BlockSpec((1, d, tn), lambda l, j: (l, 0, j)),
                  pl.BlockSpec((1, 1, tn), lambda l, j: (l, 0, j))],
        out_specs=pl.BlockSpec((1, 8, tn), lambda l, j: (l, 0, j)),
        out_shape=jax.ShapeDtypeStruct((depth, 8, w6), F32),
        compiler_params=_params(("parallel", "parallel")),
        name="modulation",
    )(cin, w_mod, b_mod.reshape(depth, 1, w6))


def _rope_slab(t, cos, sin_up, sin_dn):
    return t * cos + pltpu.roll(t, LANES - AX_PAIRS, 1) * sin_up + pltpu.roll(t, AX_PAIRS, 1) * sin_dn


def _col_scale(s0):
    if OFF_RK <= s0 < OFF_RV:
        return HEAD_DIM ** -0.5
    if OFF_AQ <= s0 < OFF_AK:
        return HEAD_DIM ** -0.5 * LOG2E
    return None


def _inproj_kernel(x_ref, mod_ref, g_ref, w_ref, cosr_ref, supr_ref, sdnr_ref, cosc_ref, supc_ref, sdnc_ref,
                   kdf_ref, kdb_ref, cdf_ref, s0f_ref, o_ref, sf_ref, ub_ref, ff_ref, stf, *, rope):
    tm = x_ref.shape[1]
    nh = max(tm // HALF_ROWS, 1)
    hr = tm // nh
    rope_cols = ((OFF_RQ, OFF_RV), (OFF_AQ, OFF_AV))
    step = 2 * LANES
    zs, rot = {}, {}

    def rope_tiles(i):
        g0 = i * hr // GRID_W
        rot[i] = [jnp.concatenate([r_ref[g0 + g:g0 + g + 1, :] + c_ref[...] for g in range(hr // GRID_W)], axis=0)
                  for r_ref, c_ref in ((cosr_ref, cosc_ref), (supr_ref, supc_ref), (sdnr_ref, sdnc_ref))]

    def norm(i):
        if rope:
            rope_tiles(i)
        x = x_ref[0, i * hr:(i + 1) * hr]
        y = x * lax.rsqrt(jnp.mean(x * x, axis=-1, keepdims=True) + EPS) * g_ref[...]
        zs[i] = (y * (1.0 + mod_ref[0, 1:2, :]) + mod_ref[0, 0:1, :]).astype(BF16)

    def project(i, c0):
        rows = slice(i * hr, (i + 1) * hr)
        acc = jnp.dot(zs[i], w_ref[:, c0:c0 + step], preferred_element_type=F32)
        for s0 in range(c0, c0 + step, LANES):
            t = acc[:, s0 - c0:s0 - c0 + LANES]
            if rope and any(a <= s0 < b for a, b in rope_cols):
                t = _rope_slab(t, *rot[i])
            if _col_scale(s0) is not None:
                t = t * _col_scale(s0)
            o_ref[0, rows, s0:s0 + LANES] = t

    lo = _lo_half_mask((HEAD_DIM, LANES))

    @pl.when(pl.program_id(1) == 0)
    def _():
        stf[...] = s0f_ref[0]

    def increment(k, v):
        kdf = (k * kdf_ref[...]).astype(BF16)
        kdb = (k * kdb_ref[...]).astype(BF16)
        fwd, bwd = [], []
        for j in range(RET_W // LANES):
            sl = slice(j * LANES, (j + 1) * LANES)
            kd = jnp.concatenate([kdf[:, sl], kdb[:, sl]], axis=1)
            u = lax.dot_general(kd, v[:, sl], (((0,), (0,)), ((), ())), preferred_element_type=F32)
            fwd.append(jnp.where(lo, u[:HEAD_DIM], u[HEAD_DIM:LANES]))
            bwd.append(jnp.where(lo, u[LANES:LANES + HEAD_DIM], u[LANES + HEAD_DIM:]))
        return jnp.concatenate(fwd, axis=1), jnp.concatenate(bwd, axis=1)

    def increments(i):
        st = stf[...]
        for r in range(i * hr // CHUNK, (i + 1) * hr // CHUNK):
            rows = slice(r * CHUNK, (r + 1) * CHUNK)
            k = o_ref[0, rows, OFF_RK:OFF_RK + RET_W]
            v = o_ref[0, rows, OFF_RV:OFF_RV + RET_W].astype(BF16)
            sf_ref[0, r] = st.astype(sf_ref.dtype)
            uf, ub_ref[0, r] = increment(k, v)
            st = st * cdf_ref[...] + uf
        stf[...] = st

    chunks = list(range(0, IN_W, step))
    norm(0)
    for i in range(nh):
        for n_c, c0 in enumerate(chunks):
            project(i, c0)
            if n_c == 1 and i + 1 < nh:
                norm(i + 1)
        increments(i)

    @pl.when(pl.program_id(1) == pl.num_programs(1) - 1)
    def _():
        ff_ref[0] = stf[...]


def _inproj(x, l, mod, stream_of, g, w, rope_tabs, tabs, s0f, *, rope, tm):
    b, n, d = x.shape
    layer = lambda *shape: pl.BlockSpec((None,) + shape, lambda bi, i: (l,) + (0,) * len(shape),
                                        pipeline_mode=pl.Buffered(1))
    nc, cpt = n // CHUNK, tm // CHUNK
    row_tab = pl.BlockSpec((tm // GRID_W, LANES), lambda bi, i: (i, 0))
    col_tab = pl.BlockSpec((GRID_W, LANES), lambda bi, i: (0, 0), pipeline_mode=pl.Buffered(1))
    u_spec = pl.BlockSpec((1, cpt, HEAD_DIM, RET_W), lambda bi, i: (bi, i, 0, 0))
    st_spec = pl.BlockSpec((1, HEAD_DIM, RET_W), lambda bi, i: (bi, 0, 0))
    return pl.pallas_call(
        functools.partial(_inproj_kernel, rope=rope),
        grid=(b, n // tm),
        in_specs=[pl.BlockSpec((1, tm, d), lambda bi, i: (bi, i, 0)),
                  pl.BlockSpec((None, 1, 6, d), lambda bi, i: (l, stream_of(bi), 0, 0)),
                  layer(1, d), layer(d, IN_W),
                  row_tab, row_tab, row_tab, col_tab, col_tab, col_tab,
                  layer(CHUNK, RET_W), layer(CHUNK, RET_W), layer(1, RET_W), st_spec],
        out_specs=[pl.BlockSpec((1, tm, IN_W), lambda bi, i: (bi, i, 0)), u_spec, u_spec, st_spec],
        out_shape=[jax.ShapeDtypeStruct((b, n, IN_W), F32),
                   jax.ShapeDtypeStruct((b, nc, HEAD_DIM, RET_W), BF16),
                   jax.ShapeDtypeStruct((b, nc, HEAD_DIM, RET_W), F32),
                   jax.ShapeDtypeStruct((b, HEAD_DIM, RET_W), F32)],
        scratch_shapes=[pltpu.VMEM((HEAD_DIM, RET_W), F32)],
        compiler_params=_params(("parallel", "arbitrary")),
        name="inproj_rope" if rope else "inproj_ctx",
    )(x, mod, g, w, *rope_tabs, tabs["kdf"], tabs["kdb"], tabs["cdf"], s0f)


def _mixer_kernel(*refs, local, r, l):
    if local:
        (p_ref, ckv_ref, sf_ref, ub_ref, s0b_ref, dm_ref, qdf_ref, qdb_ref, cdb_ref, rng_ref, sink_ref, cng_ref, ws_ref,
         bs_ref, kvm_ref, kvp_ref, o_ref, fb_ref, stb) = refs
    else:
        (p_ref, ckv_ref, sf_ref, ub_ref, s0b_ref, dm_ref, qdf_ref, qdb_ref, cdb_ref, rng_ref, sink_ref, cng_ref, ws_ref,
         bs_ref, o_ref, fb_ref, stb) = refs
    nsteps = pl.num_programs(1)
    step = nsteps - 1 - pl.program_id(1)
    lo = _lo_half_mask((CHUNK, LANES))
    contract_lanes = (((1,), (1,)), ((), ()))
    lane = lax.broadcasted_iota(jnp.int32, (CHUNK, RET_W), 1)
    head_of_lane = [(lane >= h * HEAD_DIM) & (lane < (h + 1) * HEAD_DIM) for h in range(RET_HEADS)]
    lane_s = lax.broadcasted_iota(jnp.int32, (HEAD_DIM, RET_W), 1)
    head_of_state_lane = [(lane_s >= h * HEAD_DIM) & (lane_s < (h + 1) * HEAD_DIM) for h in range(RET_HEADS)]
    if local:
        qi = lax.broadcasted_iota(jnp.int32, (CHUNK, CHUNK), 0)
        kj = lax.broadcasted_iota(jnp.int32, (CHUNK, CHUNK), 1)
    ctx_kv = ckv_ref[0]

    def per_head_rows(t):
        return jnp.concatenate([jnp.where(hm, t, 0.0) for hm in head_of_lane], axis=0).astype(BF16)

    def state_rows(s):
        s = s.astype(F32)
        return jnp.concatenate([jnp.where(hm, s, 0.0) for hm in head_of_state_lane], axis=0).astype(BF16)

    @pl.when(pl.program_id(1) == 0)
    def _():
        stb[...] = s0b_ref[0]

    sb, st_b = {}, stb[...]
    for t in reversed(range(r)):
        sb[t] = st_b
        st_b = st_b * cdb_ref[...] + ub_ref[0, t]
    stb[...] = st_b

    @pl.when(pl.program_id(1) == nsteps - 1)
    def _():
        fb_ref[0] = st_b

    rows_of = lambda t: slice(t * CHUNK, (t + 1) * CHUNK)
    cols_of = lambda t: (lambda off, w: p_ref[0, rows_of(t), off:off + w])
    att = {}

    def retention(t):
        cols = cols_of(t)
        rq, rk, rv = cols(OFF_RQ, RET_W), cols(OFF_RK, RET_W), cols(OFF_RV, RET_W)
        sc = lax.dot_general(rq.astype(BF16), per_head_rows(rk), contract_lanes, preferred_element_type=F32)
        sc = sc * dm_ref[...]
        lhs = jnp.concatenate([sc.astype(BF16), (rq * qdf_ref[...]).astype(BF16), (rq * qdb_ref[...]).astype(BF16)],
                              axis=1)
        rhs = jnp.concatenate([per_head_rows(rv), state_rows(sf_ref[0, t]), state_rows(sb[t])], axis=0)
        o = jnp.dot(lhs, rhs, preferred_element_type=F32)
        for j in range(RET_W // LANES):
            sl = slice(j * LANES, (j + 1) * LANES)
            oj = _group_norm64(o[:, sl], lo) * rng_ref[:, sl]
            o_ref[0, rows_of(t), sl] = (_silu(cols(OFF_RG + j * LANES, LANES)) * oj).astype(o_ref.dtype)

    def attention_scores(t):
        cols = cols_of(t)
        if local:
            kv_of = lambda u: p_ref[0, rows_of(u), OFF_AK:OFF_AK + 2 * KV_W]
            kv_prev = kvm_ref[0] if t == 0 else kv_of(t - 1)
            kv_next = kvp_ref[0] if t == r - 1 else kv_of(t + 1)
            kv = jnp.concatenate([kv_prev, kv_of(t), kv_next, ctx_kv], axis=0)
        else:
            kv = ctx_kv
        nk = kv.shape[0]
        keys = kv[:, :KV_W].astype(BF16)
        lo_k = _lo_half_mask((nk, LANES))
        vals = kv[:, KV_W:]
        vals0 = jnp.where(lo_k, vals, 1.0).astype(BF16)
        vals1 = jnp.where(lo_k, 1.0, vals).astype(BF16)
        qs, sinks = [], []
        for h in range(ATT_Q_HEADS):
            slab = cols(OFF_AQ + (h % ATT_GROUP) * LANES, LANES)
            qs.append(jnp.where(lo, slab, 0.0) if h < ATT_GROUP else jnp.where(lo, 0.0, slab))
            sinks.append(jnp.full((CHUNK, LANES), sink_ref[l, h] * LOG2E, F32))
        q = jnp.concatenate(qs, axis=0).astype(BF16)
        half = q.shape[0] // 2
        s = jnp.concatenate([lax.dot_general(q[:half], keys, contract_lanes, preferred_element_type=F32),
                             lax.dot_general(q[half:], keys, contract_lanes, preferred_element_type=F32)], axis=0)
        att[t] = (s, jnp.concatenate(sinks, axis=0), vals0, vals1)

    def attention_softmax(t):
        s, sink, vals0, vals1 = att[t]
        if local:
            has_prev = jnp.where(step > 0, 0.0, NEG) if t == 0 else 0.0
            has_next = jnp.where(step < nsteps - 1, 0.0, NEG) if t == r - 1 else 0.0
            bias_prev = jnp.concatenate([jnp.where(kj >= qi, has_prev, NEG)] * ATT_Q_HEADS, axis=0)
            bias_next = jnp.concatenate([jnp.where(kj <= qi, has_next, NEG)] * ATT_Q_HEADS, axis=0)
            s = jnp.concatenate([s[:, :CHUNK] + bias_prev, s[:, CHUNK:2 * CHUNK],
                                 s[:, 2 * CHUNK:3 * CHUNK] + bias_next, s[:, 3 * CHUNK:]], axis=1)
        m = jnp.maximum(jnp.broadcast_to(jnp.max(s, axis=-1, keepdims=True), sink.shape), sink)
        e = jnp.exp2(s - jnp.concatenate([m] * (s.shape[1] // LANES), axis=1)).astype(BF16)
        att[t] = (e, jnp.exp2(sink - m), vals0, vals1)

    def attention_values(t):
        e, esink, vals0, vals1 = att.pop(t)
        half = e.shape[0] // 2
        pv0 = jnp.dot(e[:half], vals0, preferred_element_type=F32)
        pv1 = jnp.dot(e[half:], vals1, preferred_element_type=F32)
        for u in range(ATT_GROUP):
            ru = slice(u * CHUNK, (u + 1) * CHUNK)
            a, b = pv0[ru], pv1[ru]
            num = jnp.where(lo, a, b)
            den = pltpu.roll(jnp.where(lo, b, a), HEAD_DIM, 1) + jnp.where(lo, esink[ru], esink[half:][ru])
            col = RET_W + u * LANES
            o_ref[0, rows_of(t), col:col + LANES] = (num * (1.0 / den)).astype(o_ref.dtype)

    def gmlp(t):
        cols = cols_of(t)
        u_act = _gelu_tanh(cols(OFF_CU, CM_W))
        vg = _gelu_tanh(cols(OFF_CV, CM_W))
        vn = jnp.concatenate([_group_norm64(vg[:, j * LANES:(j + 1) * LANES], lo) for j in range(CM_W // LANES)],
                             axis=1) * cng_ref[...]
        sp = jnp.dot(ws_ref[...], per_head_rows(vn), preferred_element_type=F32) + bs_ref[...]
        o_ref[0, rows_of(t), RET_W + ATT_W:] = (u_act * sp).astype(o_ref.dtype)

    for t in range(min(2, r)):
        attention_scores(t)
    for t in range(r):
        attention_softmax(t)
        retention(t)
        if t + 2 < r:
            attention_scores(t + 2)
        attention_values(t)
        gmlp(t)


def _mixer(p, pc, sf, ub, s0b, l, tabs, rng, sink, cng, ws, bs, *, local, r):
    b, n, _ = p.shape
    m = pc.shape[1]
    ns = n // (r * CHUNK)
    kv_col = OFF_AK // (2 * KV_W)
    last_chunk = n // CHUNK - 1
    layer = lambda rr, w: pl.BlockSpec((None, rr, w), lambda bi, c: (l, 0, 0), pipeline_mode=pl.Buffered(1))
    blk = lambda c: ns - 1 - c
    st_spec = pl.BlockSpec((1, HEAD_DIM, RET_W), lambda bi, c: (bi, 0, 0))
    in_specs = [pl.BlockSpec((1, r * CHUNK, IN_W), lambda bi, c: (bi, blk(c), 0)),
                pl.BlockSpec((1, m, 2 * KV_W), lambda bi, c: (bi, 0, kv_col)),
                pl.BlockSpec((1, r, HEAD_DIM, RET_W), lambda bi, c: (bi, blk(c), 0, 0)),
                pl.BlockSpec((1, r, HEAD_DIM, RET_W), lambda bi, c: (bi, blk(c), 0, 0)),
                st_spec,
                layer(CHUNK, RET_HEADS * CHUNK),
                layer(CHUNK, RET_W), layer(CHUNK, RET_W), layer(1, RET_W), layer(1, RET_W),
                pl.BlockSpec(memory_space=pltpu.SMEM),
                layer(1, CM_W),
                layer(CHUNK, CM_GROUPS * CHUNK),
                layer(CHUNK, CM_W)]
    args = [p, pc, sf, ub, s0b, tabs["dm"], tabs["qdf"], tabs["qdb"], tabs["cdb"], rng, sink, cng, ws, bs]
    if local:
        in_specs += [pl.BlockSpec((1, CHUNK, 2 * KV_W), lambda bi, c: (bi, jnp.maximum(blk(c) * r - 1, 0), kv_col)),
                     pl.BlockSpec((1, CHUNK, 2 * KV_W),
                                  lambda bi, c: (bi, jnp.minimum((blk(c) + 1) * r, last_chunk), kv_col))]
        args += [p, p]
    return pl.pallas_call(
        functools.partial(_mixer_kernel, local=local, r=r, l=l),
        grid=(b, ns),
        in_specs=in_specs,
        out_specs=[pl.BlockSpec((1, r * CHUNK, MIX_W), lambda bi, c: (bi, blk(c), 0)), st_spec],
        out_shape=[jax.ShapeDtypeStruct((b, n, MIX_W), BF16), jax.ShapeDtypeStruct((b, HEAD_DIM, RET_W), F32)],
        scratch_shapes=[pltpu.VMEM((HEAD_DIM, RET_W), F32)],
        compiler_params=_params(("parallel", "arbitrary")),
        name="mixer_local" if local else "mixer_ctx",
    )(*args)


def _outffn_kernel(x_ref, mix_ref, mod_ref, g_ref, wo_ref, wg_ref, wu_ref, wd_ref, fg_ref, o_ref, *, th, final):
    tm = x_ref.shape[1]
    nh = max(tm // HALF_ROWS, 1)
    hr = tm // nh
    rows = [slice(i * hr, (i + 1) * hr) for i in range(nh)]
    hid = wg_ref.shape[1]
    nsl = hid // th
    x1s, zs, accs = {}, {}, {}

    def outproj(i):
        a = jnp.dot(mix_ref[0, rows[i]], wo_ref[...], preferred_element_type=F32)
        x1s[i] = x_ref[0, rows[i]] + mod_ref[0, 2:3, :] * a

    def norm(i):
        x1 = x1s[i]
        y = x1 * lax.rsqrt(jnp.mean(x1 * x1, axis=-1, keepdims=True) + EPS) * g_ref[...]
        zs[i] = (y * (1.0 + mod_ref[0, 4:5, :]) + mod_ref[0, 3:4, :]).astype(BF16)
        accs[i] = jnp.zeros(x1.shape, F32)

    def ffn_slice(i, k):
        h0 = k * th
        hg = jnp.dot(zs[i], wg_ref[:, h0:h0 + th], preferred_element_type=F32)
        hu = jnp.dot(zs[i], wu_ref[:, h0:h0 + th], preferred_element_type=F32)
        accs[i] = accs[i] + jnp.dot((_silu(hg) * hu).astype(BF16), wd_ref[h0:h0 + th, :], preferred_element_type=F32)

    def finish(i):
        x2 = x1s[i] + mod_ref[0, 5:6, :] * accs[i]
        if final:
            x2 = x2 * lax.rsqrt(jnp.mean(x2 * x2, axis=-1, keepdims=True) + EPS) * fg_ref[...]
        o_ref[0, rows[i]] = x2

    for i in range(nh):
        outproj(i)
    norm(0)
    for i in range(nh):
        for k in range(nsl):
            ffn_slice(i, k)
            if k == 1 and i + 1 < nh:
                norm(i + 1)
        finish(i)


def _outffn(x, mix, l, mod, stream_of, g, wo, wg, wu, wd, fg, *, tm, final):
    b, n, d = x.shape
    hid = wg.shape[-1]
    layer = lambda *shape: pl.BlockSpec((None,) + shape, lambda bi, i: (l,) + (0,) * len(shape),
                                        pipeline_mode=pl.Buffered(1))
    return pl.pallas_call(
        functools.partial(_outffn_kernel, th=256, final=final),
        grid=(b, n // tm),
        in_specs=[pl.BlockSpec((1, tm, d), lambda bi, i: (bi, i, 0)),
                  pl.BlockSpec((1, tm, MIX_W), lambda bi, i: (bi, i, 0)),
                  pl.BlockSpec((None, 1, 6, d), lambda bi, i: (l, stream_of(bi), 0, 0)),
                  layer(1, d), layer(MIX_W, d), layer(d, hid), layer(d, hid), layer(hid, d),
                  pl.BlockSpec((1, d), lambda bi, i: (0, 0))],
        out_specs=pl.BlockSpec((1, tm, d), lambda bi, i: (bi, i, 0)),
        out_shape=jax.ShapeDtypeStruct((b, n, d), F32),
        compiler_params=_params(("parallel", "parallel")),
        name="outffn_final" if final else "outffn",
    )(x, mix, mod, g, wo, wg, wu, wd, fg.reshape(1, d))


def _rope_tables(n):
    rows = n // GRID_W
    lane = jnp.arange(LANES)
    inv = 1.0 / (ROPE_BASE ** ((lane % AX_PAIRS).astype(F32) / AX_PAIRS))
    row_lane = ((lane % HEAD_DIM) < 2 * AX_PAIRS)[None, :]
    first = ((lane % (2 * AX_PAIRS)) < AX_PAIRS)[None, :]
    ang_r = jnp.arange(rows, dtype=F32)[:, None] * inv[None, :]
    ang_c = jnp.arange(GRID_W, dtype=F32)[:, None] * inv[None, :]
    on_r = lambda f: jnp.where(row_lane, f, 0.0)
    on_c = lambda f: jnp.where(row_lane, 0.0, f)
    return (on_r(jnp.cos(ang_r)), on_r(jnp.where(first, -jnp.sin(ang_r), 0.0)), on_r(jnp.where(first, 0.0, jnp.sin(ang_r))),
            on_c(jnp.cos(ang_c)), on_c(jnp.where(first, -jnp.sin(ang_c), 0.0)), on_c(jnp.where(first, 0.0, jnp.sin(ang_c))))


def _decay_tables(decay_f, decay_b):
    lg_f = jax.nn.log_sigmoid(decay_f.astype(F32))
    lg_b = jax.nn.log_sigmoid(decay_b.astype(F32))
    depth = lg_f.shape[0]
    idx = jnp.arange(CHUNK, dtype=F32)
    diff = idx[:, None] - idx[None, :]
    intra = lambda lg, dd: jnp.where(dd >= 0, jnp.exp(lg[:, :, None, None] * jnp.maximum(dd, 0.0)), 0.0)
    wide = lambda t: jnp.repeat(jnp.swapaxes(t, 1, 2), HEAD_DIM, axis=2)
    return {
        "dm": jnp.swapaxes(intra(lg_f, diff) + intra(lg_b, -diff), 1, 2).reshape(depth, CHUNK, RET_HEADS * CHUNK),
        "qdf": wide(jnp.exp(lg_f[:, :, None] * (idx + 1.0))),
        "qdb": wide(jnp.exp(lg_b[:, :, None] * (CHUNK - idx))),
        "kdf": wide(jnp.exp(lg_f[:, :, None] * (CHUNK - 1.0 - idx))),
        "kdb": wide(jnp.exp(lg_b[:, :, None] * idx)),
        "cdf": wide(jnp.exp(lg_f[:, :, None] * CHUNK)),
        "cdb": wide(jnp.exp(lg_b[:, :, None] * CHUNK)),
    }


def kernel(x, c, ctx, c_ctx, w_mod, b_mod, norm1_g, norm2_g, w_in, ret_decay_f, ret_decay_b, ret_norm_g, attn_sink,
           cm_norm_g, cm_w_s, cm_b_s, w_out, w_gate, w_up, w_down, final_norm_g):
    bsz, n, d = x.shape
    m = ctx.shape[1]
    depth = w_in.shape[0]
    assert n % CHUNK == 0 and m % CHUNK == 0 and d % LANES == 0 and bsz + 1 <= 8
    tm = next(t for t in (2 * HALF_ROWS, HALF_ROWS, CHUNK) if n % t == 0)

    cin = jnp.zeros((8, d), F32).at[:bsz].set(c).at[bsz].set(c_ctx)
    mod = _modulation(cin, w_mod, b_mod).reshape(depth, 8, 6, d)
    rope = _rope_tables(n)
    mc = bsz * m
    no_rope = tuple(jnp.zeros((r, LANES), F32) for r in (m // GRID_W,) * 3 + (GRID_W,) * 3)
    zero_state = jnp.zeros((bsz, HEAD_DIM, RET_W), F32)
    latent_stream = lambda bi: bi
    ctx_stream = lambda bi: bsz

    w_in_c, w_out_c = w_in.astype(BF16), w_out.astype(BF16)
    aq = w_in_c[:, :, OFF_AQ:OFF_AK].reshape(depth, d, ATT_KV_HEADS, ATT_GROUP, HEAD_DIM)
    aq = jnp.swapaxes(aq, 2, 3).reshape(depth, d, ATT_W)
    w_in_b = jnp.concatenate([w_in_c[:, :, :OFF_AQ], aq, w_in_c[:, :, OFF_AK:]], axis=2)
    att = w_out_c[:, RET_W:RET_W + ATT_W].reshape(depth, ATT_KV_HEADS, ATT_GROUP, HEAD_DIM, d)
    att = jnp.swapaxes(att, 1, 2).reshape(depth, ATT_W, d)
    wo_b = jnp.concatenate([w_out_c[:, :RET_W], att, w_out_c[:, RET_W + ATT_W:]], axis=1)
    wg_b, wu_b, wd_b = w_gate.astype(BF16), w_up.astype(BF16), w_down.astype(BF16)

    tabs = _decay_tables(ret_decay_f, ret_decay_b)
    g1, g2 = norm1_g.reshape(depth, 1, d), norm2_g.reshape(depth, 1, d)
    rng, cng = ret_norm_g.reshape(depth, 1, RET_W), cm_norm_g.reshape(depth, 1, CM_W)
    ws = jnp.swapaxes(cm_w_s, 1, 2).reshape(depth, CHUNK, CM_GROUPS * CHUNK).astype(BF16)
    bs = jnp.repeat(jnp.swapaxes(cm_b_s, 1, 2), HEAD_DIM, axis=2)
    mix_args = (tabs, rng, attn_sink, cng, ws, bs)
    ffn_w = (wo_b, wg_b, wu_b, wd_b, final_norm_g)

    h = ctx
    for l in range(depth):
        last = l == depth - 1
        pc, sfc, ubc, fin_f = _inproj(h, l, mod, ctx_stream, g1, w_in_b, no_rope, tabs, zero_state, rope=False, tm=m)
        mixc, fin_b = _mixer(pc, pc, sfc, ubc, zero_state, l, *mix_args, local=False, r=m // CHUNK)
        p, sf, ub, _ = _inproj(x, l, mod, latent_stream, g1, w_in_b, rope, tabs, fin_f, rope=True, tm=tm)
        mix, _ = _mixer(p, pc, sf, ub, fin_b, l, *mix_args, local=True, r=4 if n % (4 * CHUNK) == 0 else 1)
        x = _outffn(x, mix, l, mod, latent_stream, g2, *ffn_w, tm=tm, final=last)
        if not last:
            h = _outffn(h.reshape(1, mc, d), mixc.reshape(1, mc, MIX_W), l, mod, ctx_stream, g2, *ffn_w, tm=mc,
                        final=False).reshape(bsz, m, d)
    return x
```
